```python
import math
import jax, jax.numpy as jnp
from jax import lax
import numpy as np

D_MODEL = 1024
BATCH = 2
SEQ = 8192
DEPTH = 4
DEC_BATCH = 32
DEC_SEQ = 4
PAST_LEN = 8192
PAGE_SIZE = 128

N_MIXERS = 3
D_FF = -(-8 * D_MODEL // (3 * 256)) * 256
EPS = 1e-6
CONV_W = 4
LRU_WIDTH = D_MODEL
LRU_BLOCKS = 4
LRU_BW = LRU_WIDTH // LRU_BLOCKS
LRU_C = 8.0
NSA_HEADS = 16
NSA_HEAD_DIM = D_MODEL // NSA_HEADS
NSA_GROUPS = 4
NSA_HPG = NSA_HEADS // NSA_GROUPS
NSA_KVW = NSA_GROUPS * NSA_HEAD_DIM
NSA_BLOCK = 64
NSA_TOP_N = 16
NSA_WINDOW = 512
NSA_Q_BLOCK = 128
NSA_FORCE = 1e4
NSA_SCALE = NSA_HEAD_DIM ** -0.5
GDN_HEADS = 8
GDN_HEAD_DIM = D_MODEL // GDN_HEADS
GDN_WIDTH = GDN_HEADS * GDN_HEAD_DIM
GDN_CHUNK = 64
F32 = jnp.float32

kernel_name = 'hybrid_rglru_nsa_gdn_decode_step'


def rmsnorm(x, w):
    xf = x.astype(F32)
    y = xf * lax.rsqrt(jnp.mean(xf * xf, axis=-1, keepdims=True) + EPS)
    return (y * w.astype(F32)).astype(x.dtype)


def l2norm(x):
    return x * lax.rsqrt(jnp.sum(x * x, axis=-1, keepdims=True) + EPS)


def masked_softmax(s, mask):
    s = jnp.where(mask, s.astype(F32), -1e30)
    m = jnp.max(s, axis=-1, keepdims=True)
    p = jnp.where(mask, jnp.exp(s - m), 0.0)
    return p / jnp.maximum(jnp.sum(p, axis=-1, keepdims=True), 1e-30)


def causal_conv(x, buf, w):
    t = x.shape[1]
    k = w.shape[0]
    xc = jnp.concatenate([buf.astype(x.dtype), x], axis=1)
    y = sum(xc[:, j:j + t] * w[j] for j in range(k))
    return y, xc[:, xc.shape[1] - (k - 1):]


def swiglu(x, w_gate_up, w_down):
    gu = x @ w_gate_up
    return (jax.nn.silu(gu[..., :D_FF]) * gu[..., D_FF:]) @ w_down


def _lin_combine(e1, e2):
    a1, b1 = e1
    a2, b2 = e2
    return a1 * a2, a2 * b1 + b2


def rglru_mixer(xn, conv_buf, h0, w_in, conv_w, conv_b, gate_w, gate_b, lam, w_out):
    b, t, _ = xn.shape
    xy = xn @ w_in
    xb, yb = xy[..., :LRU_WIDTH], xy[..., LRU_WIDTH:]
    xc, new_buf = causal_conv(xb, conv_buf, conv_w)
    xc = (xc + conv_b).astype(F32)
    gates = jnp.einsum('btnc,kncd->kbtnd', xc.reshape(b, t, LRU_BLOCKS, LRU_BW), gate_w.astype(F32))
    gates = gates.reshape(2, b, t, LRU_WIDTH) + gate_b.astype(F32)[:, None, None, :]
    r = jax.nn.sigmoid(gates[0])
    i = jax.nn.sigmoid(gates[1])
    log_a = -LRU_C * jax.nn.softplus(-lam.astype(F32)) * r
    a = jnp.exp(log_a)
    u = jnp.sqrt(-jnp.expm1(2.0 * log_a)) * (i * xc)
    u = u.at[:, 0].add(a[:, 0] * h0.astype(F32))
    _, h = lax.associative_scan(_lin_combine, (a, u), axis=1)
    y = (h * jax.nn.gelu(yb.astype(F32))).astype(xn.dtype)
    return y @ w_out, new_buf, h[:, -1].astype(xn.dtype)


def nsa_project(xn, w_in):
    b, t, _ = xn.shape
    proj = xn @ w_in
    nq = NSA_HEADS * NSA_HEAD_DIM
    q = proj[..., :nq].reshape(b, t, NSA_GROUPS, NSA_HPG, NSA_HEAD_DIM)
    kv = proj[..., nq:nq + 6 * NSA_KVW].reshape(b, t, 6, NSA_GROUPS, NSA_HEAD_DIM)
    gates = jax.nn.sigmoid(proj[..., nq + 6 * NSA_KVW:].astype(F32)).reshape(b, t, NSA_GROUPS, NSA_HPG, 3)
    return q, kv, gates


def nsa_blocks(kv4, cmp_pe, cmp_w):
    b, l = kv4.shape[:2]
    nb = -(-l // NSA_BLOCK)
    kv4 = jnp.pad(kv4, ((0, 0), (0, nb * NSA_BLOCK - l), (0, 0), (0, 0), (0, 0)))
    blk = kv4.reshape(b, nb, NSA_BLOCK, 4, NSA_GROUPS, NSA_HEAD_DIM)
    comp = jnp.einsum('bnpcgd,cpde->bncge', blk[:, :, :, :2] + cmp_pe[:, :, None, :], cmp_w)
    sel = jnp.moveaxis(blk[:, :, :, 2:], (3, 4), (1, 2))
    return comp[:, :, 0], comp[:, :, 1], sel[:, 0], sel[:, 1]


def nsa_attend(q, gates, q_pos, ck, cv, ks, vs, kw, vw, kw_pos):
    b, nq = q.shape[:2]
    nb = ck.shape[1]
    blk = jnp.arange(nb)
    s = jnp.einsum('bqghd,bngd->bqghn', q, ck) * NSA_SCALE
    ok_c = ((blk[None, :] + 1) * NSA_BLOCK - 1 <= q_pos[:, None])[None, :, None, None, :]
    p_cmp = masked_softmax(s, ok_c)
    o_cmp = jnp.einsum('bqghn,bngd->bqghd', p_cmp, cv.astype(F32))
    cur = (q_pos // NSA_BLOCK)[:, None]
    future = (blk[None, :] > cur)[:, None, :]
    forced = ((blk[None, :] == 0) | (blk[None, :] == cur) | (blk[None, :] == cur - 1))[:, None, :]
    score = jnp.where(future, -1.0, jnp.where(forced, NSA_FORCE, jnp.sum(p_cmp, axis=3)))
    n_sel = min(NSA_TOP_N, nb)
    _, idx = lax.top_k(score, n_sel)
    bi = jnp.arange(b)[:, None, None, None]
    gi = jnp.arange(NSA_GROUPS)[None, None, :, None]
    k_sel = ks[bi, gi, idx].reshape(b, nq, NSA_GROUPS, n_sel * NSA_BLOCK, NSA_HEAD_DIM)
    v_sel = vs[bi, gi, idx].reshape(b, nq, NSA_GROUPS, n_sel * NSA_BLOCK, NSA_HEAD_DIM)
    tok = (idx[..., None] * NSA_BLOCK + jnp.arange(NSA_BLOCK)).reshape(b, nq, NSA_GROUPS, n_sel * NSA_BLOCK)
    s = jnp.einsum('bqghd,bqgkd->bqghk', q, k_sel) * NSA_SCALE
    p_slc = masked_softmax(s, (tok <= q_pos[None, :, None, None])[:, :, :, None, :])
    o_slc = jnp.einsum('bqghk,bqgkd->bqghd', p_slc, v_sel.astype(F32))
    s = jnp.einsum('bqghd,bwgd->bqghw', q, kw) * NSA_SCALE
    dist = q_pos[:, None] - kw_pos[None, :]
    ok_w = ((dist >= 0) & (dist < NSA_WINDOW) & (kw_pos[None, :] >= 0))[None, :, None, None, :]
    o_win = jnp.einsum('bqghw,bwgd->bqghd', masked_softmax(s, ok_w), vw.astype(F32))
    o = gates[..., 0:1] * o_cmp + gates[..., 1:2] * o_slc + gates[..., 2:3] * o_win
    return o.reshape(b, nq, NSA_HEADS * NSA_HEAD_DIM)


def nsa_prompt(xn, w_in, cmp_pe, cmp_w, w_out):
    b, t, _ = xn.shape
    q, kv, gates = nsa_project(xn, w_in)
    ck, cv, ks, vs = nsa_blocks(kv[:, :, :4], cmp_pe, cmp_w)
    win_pad = jnp.pad(kv[:, :, 4:], ((0, 0), (NSA_WINDOW, 0), (0, 0), (0, 0), (0, 0)))
    qb = min(NSA_Q_BLOCK, t)

    def one_block(i):
        start = i * qb
        q_b = lax.dynamic_slice_in_dim(q, start, qb, axis=1)
        g_b = lax.dynamic_slice_in_dim(gates, start, qb, axis=1)
        w_b = lax.dynamic_slice_in_dim(win_pad, start, qb + NSA_WINDOW, axis=1)
        q_pos = start + jnp.arange(qb)
        kw_pos = start - NSA_WINDOW + jnp.arange(qb + NSA_WINDOW)
        return nsa_attend(q_b, g_b, q_pos, ck, cv, ks, vs, w_b[:, :, 0], w_b[:, :, 1], kw_pos)

    o = lax.map(one_block, jnp.arange(t // qb))
    o = jnp.moveaxis(o, 0, 1).reshape(b, t, NSA_HEADS * NSA_HEAD_DIM)
    wb = min(NSA_WINDOW, t)
    return o.astype(xn.dtype) @ w_out, kv[:, :, :4], kv[:, t - wb:, 4:]


def nsa_sample(xn, cache_kv, cache_win, page_table, w_in, cmp_pe, cmp_w, w_out):
    b, t, _ = xn.shape
    q, kv, gates = nsa_project(xn, w_in)
    n_pages = page_table.shape[1]
    past = cache_kv[page_table].reshape(b, n_pages * cache_kv.shape[1], 4, NSA_GROUPS, NSA_HEAD_DIM)
    past_len = past.shape[1]
    full = jnp.concatenate([past, kv[:, :, :4].astype(past.dtype)], axis=1)
    ck, cv, ks, vs = nsa_blocks(full, cmp_pe, cmp_w)
    win_all = jnp.concatenate([cache_win, kv[:, :, 4:].astype(cache_win.dtype)], axis=1)
    wb = cache_win.shape[1]
    q_pos = past_len + jnp.arange(t)
    kw_pos = past_len - wb + jnp.arange(wb + t)
    o = nsa_attend(q, gates, q_pos, ck, cv, ks, vs, win_all[:, :, 0], win_all[:, :, 1], kw_pos)
    return o.astype(xn.dtype) @ w_out, kv[:, :, :4], win_all[:, t:]


def gdn_chunk_step(state, inp):
    q, k, v, g, beta = inp
    c = q.shape[2]
    pos = jnp.arange(c)
    incl = pos[:, None] >= pos[None, :]
    strict = pos[:, None] > pos[None, :]
    gc = jnp.cumsum(g, axis=-1)
    decay = jnp.exp(jnp.where(incl, gc[..., :, None] - gc[..., None, :], -jnp.inf))
    kb = k * beta[..., None]
    m = jnp.where(strict, jnp.einsum('bhid,bhjd->bhij', kb, k) * decay, 0.0)
    rhs = jnp.concatenate([v * beta[..., None], kb * jnp.exp(gc)[..., None]], axis=-1)
    sol = lax.linalg.triangular_solve(m + jnp.eye(c, dtype=m.dtype), rhs, left_side=True, lower=True)
    u = sol[..., :GDN_HEAD_DIM] - jnp.einsum('bhck,bhkv->bhcv', sol[..., GDN_HEAD_DIM:], state)
    attn = jnp.where(incl, jnp.einsum('bhid,bhjd->bhij', q, k) * decay, 0.0)
    o = jnp.einsum('bhck,bhkv->bhcv', q * jnp.exp(gc)[..., None], state) + jnp.einsum('bhij,bhjv->bhiv', attn, u)
    g_last = gc[..., -1]
    state = state * jnp.exp(g_last)[..., None, None] + jnp.einsum(
        'bhck,bhcv->bhkv', k * jnp.exp(g_last[..., None] - gc)[..., None], u)
    return state, o


def gdn_mixer(xn, conv_buf, s0, w_in, conv_w, a_log, dt_bias, o_norm_w, w_out):
    b, t, _ = xn.shape
    proj = xn @ w_in
    qkv, new_buf = causal_conv(proj[..., :3 * GDN_WIDTH], conv_buf, conv_w)
    qkv = jax.nn.silu(qkv.astype(F32)).reshape(b, t, 3, GDN_HEADS, GDN_HEAD_DIM)
    q = l2norm(qkv[:, :, 0]) * GDN_HEAD_DIM ** -0.5
    k = l2norm(qkv[:, :, 1])
    v = qkv[:, :, 2]
    gate = proj[..., 3 * GDN_WIDTH:4 * GDN_WIDTH].reshape(b, t, GDN_HEADS, GDN_HEAD_DIM)
    a_in = proj[..., 4 * GDN_WIDTH:4 * GDN_WIDTH + GDN_HEADS].astype(F32)
    b_in = proj[..., 4 * GDN_WIDTH + GDN_HEADS:].astype(F32)
    g = -jnp.exp(a_log.astype(F32)) * jax.nn.softplus(a_in + dt_bias.astype(F32))
    beta = jax.nn.sigmoid(b_in)
    c = GDN_CHUNK if t % GDN_CHUNK == 0 else t
    nc = t // c

    def chunks(z):
        z = z.reshape(b, nc, c, GDN_HEADS, *z.shape[3:])
        return jnp.moveaxis(z, (1, 3), (0, 2))

    state, o = lax.scan(gdn_chunk_step, s0.astype(F32), (chunks(q), chunks(k), chunks(v), chunks(g), chunks(beta)))
    o = jnp.moveaxis(o, (0, 2), (1, 3)).reshape(b, t, GDN_HEADS, GDN_HEAD_DIM)
    o = o * lax.rsqrt(jnp.mean(o * o, axis=-1, keepdims=True) + EPS) * o_norm_w.astype(F32)
    o = o * jax.nn.silu(gate.astype(F32))
    return o.reshape(b, t, GDN_WIDTH).astype(xn.dtype) @ w_out, new_buf, state.astype(xn.dtype)


def setup_inputs(seed: int = 0) -> dict:
    key = jax.random.key(seed)
    keys = iter(jax.random.split(key, 64))

    def nk():
        return next(keys)

    def w(shape, fan_in):
        return jax.random.normal(nk(), shape, F32) * fan_in ** -0.5

    def rn(shape, s=1.0):
        return jax.random.normal(nk(), shape, F32) * s

    n_pages = PAST_LEN // PAGE_SIZE
    n_used = DEC_BATCH * n_pages
    n_pool = n_used + -(-n_used // 4)
    win_len = min(NSA_WINDOW, PAST_LEN)

    def lru_params(prefix):
        u = jax.random.uniform(nk(), (LRU_WIDTH,), F32, 0.9, 0.999)
        a0 = u ** (1.0 / LRU_C)
        return {prefix + 'w_in': w((D_MODEL, 2 * LRU_WIDTH), D_MODEL),
                prefix + 'conv_w': w((CONV_W, LRU_WIDTH), CONV_W),
                prefix + 'conv_b': rn((LRU_WIDTH,), 0.02),
                prefix + 'gate_w': w((2, LRU_BLOCKS, LRU_BW, LRU_BW), LRU_BW),
                prefix + 'gate_b': rn((2, LRU_WIDTH), 0.1),
                prefix + 'lam': jnp.log(a0) - jnp.log1p(-a0),
                prefix + 'w_out': w((LRU_WIDTH, D_MODEL), LRU_WIDTH)}

    d = {}
    d['x_prompt'] = rn((BATCH, SEQ, D_MODEL))
    d['x_sample'] = rn((DEC_BATCH, DEC_SEQ, D_MODEL))
    d['state_l0_lru_conv'] = rn((DEC_BATCH, CONV_W - 1, LRU_WIDTH))
    d['state_l0_lru_h'] = rn((DEC_BATCH, LRU_WIDTH))
    d['cache_l1_kv'] = rn((n_pool, PAGE_SIZE, 4, NSA_GROUPS, NSA_HEAD_DIM))
    d['cache_l1_win'] = rn((DEC_BATCH, win_len, 2, NSA_GROUPS, NSA_HEAD_DIM))
    d['state_l2_gdn_conv'] = rn((DEC_BATCH, CONV_W - 1, 3 * GDN_WIDTH))
    d['state_l2_gdn_s'] = rn((DEC_BATCH, GDN_HEADS, GDN_HEAD_DIM, GDN_HEAD_DIM), 0.5)
    d['state_l3_lru_conv'] = rn((DEC_BATCH, CONV_W - 1, LRU_WIDTH))
    d['state_l3_lru_h'] = rn((DEC_BATCH, LRU_WIDTH))
    d['page_table'] = jax.random.permutation(nk(), n_pool)[:n_used].reshape(DEC_BATCH, n_pages).astype(jnp.int32)
    d['norm_gains'] = 1.0 + rn((DEPTH, 4, D_MODEL), 0.1)
    d['ffn_w_gate_up'] = w((DEPTH, D_MODEL, 2 * D_FF), D_MODEL)
    d['ffn_w_down'] = w((DEPTH, D_FF, D_MODEL), D_FF)
    d.update(lru_params('l0_'))
    d['l1_w_in'] = w((D_MODEL, NSA_HEADS * NSA_HEAD_DIM + 6 * NSA_KVW + 3 * NSA_HEADS), D_MODEL)
    d['l1_cmp_pe'] = rn((NSA_BLOCK, 2, NSA_HEAD_DIM), 0.1)
    d['l1_cmp_w'] = w((2, NSA_BLOCK, NSA_HEAD_DIM, NSA_HEAD_DIM), NSA_BLOCK * NSA_HEAD_DIM)
    d['l1_w_out'] = w((NSA_HEADS * NSA_HEAD_DIM, D_MODEL), NSA_HEADS * NSA_HEAD_DIM)
    dt = jnp.exp(jax.random.uniform(nk(), (GDN_HEADS,), F32, math.log(1e-3), math.log(1e-1)))
    d['l2_w_in'] = w((D_MODEL, 4 * GDN_WIDTH + 2 * GDN_HEADS), D_MODEL)
    d['l2_conv_w'] = w((CONV_W, 3 * GDN_WIDTH), CONV_W)
    d['l2_a_log'] = jnp.log(jax.random.uniform(nk(), (GDN_HEADS,), F32, 1.0, 16.0))
    d['l2_dt_bias'] = dt + jnp.log(-jnp.expm1(-dt))
    d['l2_o_norm_w'] = 1.0 + rn((GDN_HEAD_DIM,), 0.1)
    d['l2_w_out'] = w((GDN_WIDTH, D_MODEL), GDN_WIDTH)
    d.update(lru_params('l3_'))
    return d


def reference(x_prompt, x_sample, state_l0_lru_conv, state_l0_lru_h, cache_l1_kv, cache_l1_win,
              state_l2_gdn_conv, state_l2_gdn_s, state_l3_lru_conv, state_l3_lru_h, page_table,
              norm_gains, ffn_w_gate_up, ffn_w_down,
              l0_w_in, l0_conv_w, l0_conv_b, l0_gate_w, l0_gate_b, l0_lam, l0_w_out,
              l1_w_in, l1_cmp_pe, l1_cmp_w, l1_w_out,
              l2_w_in, l2_conv_w, l2_a_log, l2_dt_bias, l2_o_norm_w, l2_w_out,
              l3_w_in, l3_conv_w, l3_conv_b, l3_gate_w, l3_gate_b, l3_lam, l3_w_out):
    lru_weights = {0: (l0_w_in, l0_conv_w, l0_conv_b, l0_gate_w, l0_gate_b, l0_lam, l0_w_out),
                   3: (l3_w_in, l3_conv_w, l3_conv_b, l3_gate_w, l3_gate_b, l3_lam, l3_w_out)}
    lru_states = {0: (state_l0_lru_conv, state_l0_lru_h), 3: (state_l3_lru_conv, state_l3_lru_h)}
    gdn_weights = (l2_w_in, l2_conv_w, l2_a_log, l2_dt_bias, l2_o_norm_w, l2_w_out)
    xp, xs = x_prompt, x_sample
    bp = xp.shape[0]
    new = {}
    for layer in range(DEPTH):
        g_mix_pre, g_mix_post = norm_gains[layer, 0], norm_gains[layer, 1]
        g_ffn_pre, g_ffn_post = norm_gains[layer, 2], norm_gains[layer, 3]
        hp = rmsnorm(xp, g_mix_pre)
        hs = rmsnorm(xs, g_mix_pre)
        kind = layer % N_MIXERS
        if kind == 0:
            wts = lru_weights[layer]
            conv_s, h_s = lru_states[layer]
            mp, conv_p_new, h_p_new = rglru_mixer(hp, jnp.zeros((bp, CONV_W - 1, LRU_WIDTH), xp.dtype),
                                                  jnp.zeros((bp, LRU_WIDTH), xp.dtype), *wts)
            ms, conv_s_new, h_s_new = rglru_mixer(hs, conv_s, h_s, *wts)
            new[layer] = (conv_p_new, h_p_new, conv_s_new, h_s_new)
        elif kind == 1:
            mp, kv_p, win_p = nsa_prompt(hp, l1_w_in, l1_cmp_pe, l1_cmp_w, l1_w_out)
            ms, kv_s, win_s = nsa_sample(hs, cache_l1_kv, cache_l1_win, page_table,
                                         l1_w_in, l1_cmp_pe, l1_cmp_w, l1_w_out)
            new[layer] = (kv_p, win_p, kv_s, win_s)
        else:
            mp, conv_p_new, s_p_new = gdn_mixer(hp, jnp.zeros((bp, CONV_W - 1, 3 * GDN_WIDTH), xp.dtype),
                                                jnp.zeros((bp, GDN_HEADS, GDN_HEAD_DIM, GDN_HEAD_DIM), xp.dtype),
                                                *gdn_weights)
            ms, conv_s_new, s_s_new = gdn_mixer(hs, state_l2_gdn_conv, state_l2_gdn_s, *gdn_weights)
            new[layer] = (conv_p_new, s_p_new, conv_s_new, s_s_new)
        xp = xp + rmsnorm(mp, g_mix_post)
        xs = xs + rmsnorm(ms, g_mix_post)
        xp = xp + rmsnorm(swiglu(rmsnorm(xp, g_ffn_pre), ffn_w_gate_up[layer], ffn_w_down[layer]), g_ffn_post)
        xs = xs + rmsnorm(swiglu(rmsnorm(xs, g_ffn_pre), ffn_w_gate_up[layer], ffn_w_down[layer]), g_ffn_post)
    return (xp, xs,
            new[0][0], new[0][1], new[0][2], new[0][3],
            new[1][0], new[1][1], new[1][2], new[1][3],
            new[2][0], new[2][1], new[2][2], new[2][3],
            new[3][0], new[3][1], new[3][2], new[3][3])
```

```python
import functools

import jax
import jax.numpy as jnp
from jax import lax
from jax.experimental import pallas as pl
from jax.experimental.pallas import tpu as pltpu

F32 = jnp.float32
BF16 = jnp.bfloat16

D_MODEL = 1024
D_FF = 2816
EPS = 1e-6
CONV_W = 4
LRU_BLOCKS = 4
LRU_BW = 256
LRU_C = 8.0
NSA_HEADS = 16
NSA_DH = 64
NSA_GROUPS = 4
NSA_HPG = 4
NSA_KVW = 256
NSA_BLOCK = 64
NSA_TOP_N = 16
NSA_WINDOW = 512
NSA_FORCE = 1e4
NSA_SCALE = NSA_DH ** -0.5
GDN_HEADS = 8
GDN_DH = 128
NEG = -1e30

VMEM_LIMIT = 52 * 1024 * 1024


def _cparams(sem):
    return pltpu.CompilerParams(dimension_semantics=sem, vmem_limit_bytes=VMEM_LIMIT)


def _dot(a, b):
    return jnp.dot(a, b, preferred_element_type=F32)


def _dot_nt(a, b):
    return lax.dot_general(a, b, (((1,), (1,)), ((), ())), preferred_element_type=F32)


def _sigmoid(x):
    return 1.0 / (1.0 + jnp.exp(-x))


def _silu(x):
    return x * _sigmoid(x)


def _softplus(x):
    return jnp.maximum(x, 0.0) + jnp.log(1.0 + jnp.exp(-jnp.abs(x)))


def _gelu_tanh(x):
    c = 0.7978845608028654
    return 0.5 * x * (1.0 + jnp.tanh(c * (x + 0.044715 * x * x * x)))


def _rms(x, g):
    return x * lax.rsqrt(jnp.mean(x * x, axis=-1, keepdims=True) + EPS) * g


def _norm_matmul_kernel(x_ref, g_ref, w_ref, o_ref, xn_ref):
    @pl.when(pl.program_id(1) == 0)
    def _():
        xn_ref[...] = _rms(x_ref[...], g_ref[...]).astype(BF16)

    o_ref[...] = _dot(xn_ref[...], w_ref[...]).astype(o_ref.dtype)


def norm_matmul(x, gain, w, tm, tn, out_dtype=F32):
    m, d = x.shape
    n = w.shape[1]
    return pl.pallas_call(
        _norm_matmul_kernel,
        grid=(m // tm, n // tn),
        in_specs=[pl.BlockSpec((tm, d), lambda i, j: (i, 0)),
                  pl.BlockSpec((1, d), lambda i, j: (0, 0)),
                  pl.BlockSpec((d, tn), lambda i, j: (0, j))],
        out_specs=pl.BlockSpec((tm, tn), lambda i, j: (i, j)),
        out_shape=jax.ShapeDtypeStruct((m, n), out_dtype),
        scratch_shapes=[pltpu.VMEM((tm, d), BF16)],
        compiler_params=_cparams(("parallel", "arbitrary")),
        name="norm_matmul",
    )(x, gain.reshape(1, d), w)


def _matmul_norm_res_kernel(y_ref, w_ref, g_ref, r_ref, o_ref):
    m = _dot(y_ref[...], w_ref[...])
    o_ref[...] = r_ref[...] + _rms(m, g_ref[...])


def matmul_norm_res(y, w, gain, res, tm):
    m, k = y.shape
    d = w.shape[1]
    return pl.pallas_call(
        _matmul_norm_res_kernel,
        grid=(m // tm,),
        in_specs=[pl.BlockSpec((tm, k), lambda i: (i, 0)),
                  pl.BlockSpec((k, d), lambda i: (0, 0)),
                  pl.BlockSpec((1, d), lambda i: (0, 0)),
                  pl.BlockSpec((tm, d), lambda i: (i, 0))],
        out_specs=pl.BlockSpec((tm, d), lambda i: (i, 0)),
        out_shape=jax.ShapeDtypeStruct((m, d), F32),
        compiler_params=_cparams(("parallel",)),
        name="matmul_norm_res",
    )(y, w, gain.reshape(1, d), res)


FFN_TF = 256


def _ffn_kernel(x_ref, gpre_ref, gpost_ref, wg_ref, wu_ref, wd_ref, o_ref, xn_ref, acc_ref):
    j = pl.program_id(1)

    @pl.when(j == 0)
    def _():
        xn_ref[...] = _rms(x_ref[...], gpre_ref[...]).astype(BF16)
        acc_ref[...] = jnp.zeros_like(acc_ref)

    xn = xn_ref[...]
    act = _silu(_dot(xn, wg_ref[...])) * _dot(xn, wu_ref[...])
    acc_ref[...] += _dot(act.astype(BF16), wd_ref[...])

    @pl.when(j == pl.num_programs(1) - 1)
    def _():
        o_ref[...] = x_ref[...] + _rms(acc_ref[...], gpost_ref[...])


def ffn(x, g_pre, g_post, w_gate_up, w_down, tm):
    m, d = x.shape
    nf = D_FF // FFN_TF
    return pl.pallas_call(
        _ffn_kernel,
        grid=(m // tm, nf),
        in_specs=[pl.BlockSpec((tm, d), lambda i, j: (i, 0)),
                  pl.BlockSpec((1, d), lambda i, j: (0, 0)),
                  pl.BlockSpec((1, d), lambda i, j: (0, 0)),
                  pl.BlockSpec((d, FFN_TF), lambda i, j: (0, j)),
                  pl.BlockSpec((d, FFN_TF), lambda i, j: (0, j + nf)),
                  pl.BlockSpec((FFN_TF, d), lambda i, j: (j, 0))],
        out_specs=pl.BlockSpec((tm, d), lambda i, j: (i, 0)),
        out_shape=jax.ShapeDtypeStruct((m, d), F32),
        scratch_shapes=[pltpu.VMEM((tm, d), BF16), pltpu.VMEM((tm, d), F32)],
        compiler_params=_cparams(("parallel", "arbitrary")),
        name="ffn",
    )(x, g_pre.reshape(1, d), g_post.reshape(1, d), w_gate_up, w_gate_up, w_down)


def _lru_kernel(xb_ref, yb_ref, buf_ref, h0_ref, cw_ref, cb_ref, gw_ref, gb_ref, lam_ref,
                y_ref, nbuf_ref, hl_ref, ext_ref, a_ref, u_ref, hc_ref, *, tt, t_last):
    i = pl.program_id(1)
    w = D_MODEL

    @pl.when(i == 0)
    def _():
        ext_ref[0:8, :] = buf_ref[0]
        hc_ref[...] = h0_ref[0]

    @pl.when(i > 0)
    def _():
        ext_ref[0:8, :] = ext_ref[tt:tt + 8, :]

    ext_ref[8:8 + tt, :] = xb_ref[0]
    cw = cw_ref[...]
    xc = cb_ref[...] + cw[0:1] * ext_ref[5:5 + tt, :]
    for j in range(1, CONV_W):
        xc = xc + cw[j:j + 1] * ext_ref[5 + j:5 + j + tt, :]
    xcb = xc.astype(BF16)
    rs, igs = [], []
    for n in range(LRU_BLOCKS):
        blk = xcb[:, n * LRU_BW:(n + 1) * LRU_BW]
        rs.append(_dot(blk, gw_ref[0, n]))
        igs.append(_dot(blk, gw_ref[1, n]))
    gb = gb_ref[...]
    r = _sigmoid(jnp.concatenate(rs, axis=1) + gb[0:1])
    ig = _sigmoid(jnp.concatenate(igs, axis=1) + gb[1:2])
    log_a = (-LRU_C * _softplus(-lam_ref[...])) * r
    a = jnp.exp(log_a)
    a_ref[...] = a
    u_ref[...] = jnp.sqrt(1.0 - a * a) * (ig * xc)

    def body(t, h):
        h = a_ref[pl.ds(t, 1), :] * h + u_ref[pl.ds(t, 1), :]
        u_ref[pl.ds(t, 1), :] = h
        return h

    hc_ref[...] = lax.fori_loop(0, tt, body, hc_ref[...], unroll=8)
    y_ref[0] = (u_ref[...] * _gelu_tanh(yb_ref[0])).astype(y_ref.dtype)

    @pl.when(i == pl.num_programs(1) - 1)
    def _():
        nbuf_ref[0] = ext_ref[8 + t_last - 3:8 + t_last, :]
        hl_ref[0] = u_ref[t_last - 1:t_last, :]


def lru_core(proj, conv_buf, h0, conv_w, conv_b, gate_w, gate_b, lam, t_valid, tt):
    b, t, _ = proj.shape
    w = D_MODEL
    nt = t // tt
    t_last = t_valid - (nt - 1) * tt
    buf8 = jnp.concatenate([jnp.zeros((b, 5, w), F32), conv_buf], axis=1)
    kern = functools.partial(_lru_kernel, tt=tt, t_last=t_last)
    return pl.pallas_call(
        kern,
        grid=(b, nt),
        in_specs=[pl.BlockSpec((1, tt, w), lambda bi, i: (bi, i, 0)),
                  pl.BlockSpec((1, tt, w), lambda bi, i: (bi, i, 1)),
                  pl.BlockSpec((1, 8, w), lambda bi, i: (bi, 0, 0)),
                  pl.BlockSpec((1, 1, w), lambda bi, i: (bi, 0, 0)),
                  pl.BlockSpec((CONV_W, w), lambda bi, i: (0, 0)),
                  pl.BlockSpec((1, w), lambda bi, i: (0, 0)),
                  pl.BlockSpec((2, LRU_BLOCKS, LRU_BW, LRU_BW), lambda bi, i: (0, 0, 0, 0)),
                  pl.BlockSpec((2, w), lambda bi, i: (0, 0)),
                  pl.BlockSpec((1, w), lambda bi, i: (0, 0))],
        out_specs=[pl.BlockSpec((1, tt, w), lambda bi, i: (bi, i, 0)),
                   pl.BlockSpec((1, 3, w), lambda bi, i: (bi, 0, 0)),
                   pl.BlockSpec((1, 1, w), lambda bi, i: (bi, 0, 0))],
        out_shape=[jax.ShapeDtypeStruct((b, t, w), BF16),
                   jax.ShapeDtypeStruct((b, 3, w), F32),
                   jax.ShapeDtypeStruct((b, 1, w), F32)],
        scratch_shapes=[pltpu.VMEM((tt + 8, w), F32), pltpu.VMEM((tt, w), F32),
                        pltpu.VMEM((tt, w), F32), pltpu.VMEM((1, w), F32)],
        compiler_params=_cparams(("parallel", "arbitrary")),
        name="lru_core",
    )(proj, proj, buf8, h0.reshape(b, 1, w), conv_w, conv_b.reshape(1, w), gate_w, gate_b,
      lam.reshape(1, w))


def lru_mixer(x, res_gain_pre, res_gain_post, conv_buf, h0, wts, tm, tt):
    w_in, conv_w, conv_b, gate_w, gate_b, lam, w_out = wts
    b, t, d = x.shape
    x2 = x.reshape(b * t, d)
    proj = norm_matmul(x2, res_gain_pre, w_in, tm, 512)
    tp = -(-t // 8) * 8
    proj3 = proj.reshape(b, t, 2 * d)
    if tp != t:
        proj3 = jnp.pad(proj3, ((0, 0), (0, tp - t), (0, 0)))
    y, nbuf, hl = lru_core(proj3, conv_buf, h0, conv_w, conv_b, gate_w, gate_b, lam, t, min(tt, tp))
    if tp != t:
        y = y[:, :t]
    out = matmul_norm_res(y.reshape(b * t, d), w_out, res_gain_post, x2, tm)
    return out.reshape(b, t, d), nbuf, hl.reshape(b, d)


GDN_W = GDN_HEADS * GDN_DH
GDN_PROJ = 4608
GDN_AB_BLOCK = 4 * GDN_W // 128
GDN_CHUNK = 128


def _gdn_prep_kernel(x_ref, ab_ref, buf_ref, cw_ref, ad_ref, q_ref, k_ref, v_ref, gb_ref, nbuf_ref,
                     ext_ref, *, tt, t_last):
    i = pl.program_id(1)

    @pl.when(i == 0)
    def _():
        ext_ref[0:8, :] = buf_ref[0]

    @pl.when(i > 0)
    def _():
        ext_ref[0:8, :] = ext_ref[tt:tt + 8, :]

    ext_ref[8:8 + tt, :] = x_ref[0]
    cw = cw_ref[...]
    xc = cw[0:1] * ext_ref[5:5 + tt, :]
    for j in range(1, CONV_W):
        xc = xc + cw[j:j + 1] * ext_ref[5 + j:5 + j + tt, :]
    xc = _silu(xc)
    for h in range(GDN_HEADS):
        qh = xc[:, h * GDN_DH:(h + 1) * GDN_DH]
        kh = xc[:, GDN_W + h * GDN_DH:GDN_W + (h + 1) * GDN_DH]
        qn = qh * lax.rsqrt(jnp.sum(qh * qh, axis=-1, keepdims=True) + EPS) * (GDN_DH ** -0.5)
        kn = kh * lax.rsqrt(jnp.sum(kh * kh, axis=-1, keepdims=True) + EPS)
        q_ref[0, :, h * GDN_DH:(h + 1) * GDN_DH] = qn
        k_ref[0, :, h * GDN_DH:(h + 1) * GDN_DH] = kn
    v_ref[0] = xc[:, 2 * GDN_W:]
    ab = ab_ref[0]
    ad = ad_ref[...]
    lane = lax.broadcasted_iota(jnp.int32, ab.shape, 1)
    g = -jnp.exp(ad[0:1]) * _softplus(ab + ad[1:2])
    gb_ref[0] = jnp.where(lane < GDN_HEADS, g, _sigmoid(ab))

    @pl.when(i == pl.num_programs(1) - 1)
    def _():
        nbuf_ref[0] = ext_ref[8 + t_last - 3:8 + t_last, :]


def gdn_prep(proj, conv_buf, conv_w, a_log, dt_bias, t_valid, tt):
    b, t, _ = proj.shape
    w3 = 3 * GDN_W
    nt = t // tt
    t_last = t_valid - (nt - 1) * tt
    buf8 = jnp.concatenate([jnp.zeros((b, 5, w3), F32), conv_buf], axis=1)
    ad = jnp.zeros((2, 128), F32).at[0, :GDN_HEADS].set(a_log).at[1, :GDN_HEADS].set(dt_bias)
    kern = functools.partial(_gdn_prep_kernel, tt=tt, t_last=t_last)
    act = jax.ShapeDtypeStruct((b, t, GDN_W), F32)
    return pl.pallas_call(
        kern,
        grid=(b, nt),
        in_specs=[pl.BlockSpec((1, tt, w3), lambda bi, i: (bi, i, 0)),
                  pl.BlockSpec((1, tt, 128), lambda bi, i: (bi, i, GDN_AB_BLOCK)),
                  pl.BlockSpec((1, 8, w3), lambda bi, i: (bi, 0, 0)),
                  pl.BlockSpec((CONV_W, w3), lambda bi, i: (0, 0)),
                  pl.BlockSpec((2, 128), lambda bi, i: (0, 0))],
        out_specs=[pl.BlockSpec((1, tt, GDN_W), lambda bi, i: (bi, i, 0)),
                   pl.BlockSpec((1, tt, GDN_W), lambda bi, i: (bi, i, 0)),
                   pl.BlockSpec((1, tt, GDN_W), lambda bi, i: (bi, i, 0)),
                   pl.BlockSpec((1, tt, 128), lambda bi, i: (bi, i, 0)),
                   pl.BlockSpec((1, 3, w3), lambda bi, i: (bi, 0, 0))],
        out_shape=[act, act, act, jax.ShapeDtypeStruct((b, t, 128), F32),
                   jax.ShapeDtypeStruct((b, 3, w3), F32)],
        scratch_shapes=[pltpu.VMEM((tt + 8, w3), F32)],
        compiler_params=_cparams(("parallel", "arbitrary")),
        name="gdn_prep",
    )(proj, proj, buf8, conv_w, ad)


def _gdn_out(o, gate, onw):
    on = o * lax.rsqrt(jnp.mean(o * o, axis=-1, keepdims=True) + EPS) * onw
    return on * _silu(gate)


def _unit_lower_inverse(m_strict, row, col):
    c = m_strict.shape[0]
    base = 16
    eye = (row == col).astype(F32)
    same = (row // base) == (col // base)
    p = -jnp.where(same, m_strict, 0.0)
    t = eye + p
    s = 1
    while 2 * s < base:
        p = _dot(p.astype(BF16), p.astype(BF16))
        t = t + _dot(t.astype(BF16), p.astype(BF16))
        s *= 2
    bs = base
    while bs < c:
        off = ((row // (2 * bs)) == (col // (2 * bs))) & ((row // bs) != (col // bs))
        m_off = jnp.where(off, m_strict, 0.0)
        tm = _dot(t.astype(BF16), m_off.astype(BF16))
        t = t - _dot(tm.astype(BF16), t.astype(BF16))
        bs *= 2
    return t


def _gdn_chunk_kernel(q_ref, k_ref, v_ref, gb_ref, gate_ref, onw_ref, s0_ref, y_ref, sl_ref, s_ref):
    i = pl.program_id(1)
    c = GDN_CHUNK

    @pl.when(i == 0)
    def _():
        s_ref[...] = s0_ref[0]

    gb = gb_ref[0]
    row = lax.broadcasted_iota(jnp.int32, (c, c), 0)
    col = lax.broadcasted_iota(jnp.int32, (c, c), 1)
    gc = gb
    sft = 1
    while sft < c:
        gc = gc + jnp.where(row >= sft, pltpu.roll(gc, sft, axis=0), 0.0)
        sft *= 2
    gct = gc.T
    incl = row >= col
    strict = row > col
    onw = onw_ref[...]
    for h in range(GDN_HEADS):
        sl = slice(h * GDN_DH, (h + 1) * GDN_DH)
        gcol = gc[:, h:h + 1]
        grow = gct[h:h + 1, :]
        decay = jnp.exp(jnp.where(incl, gcol - grow, NEG))
        beta = gb[:, GDN_HEADS + h:GDN_HEADS + h + 1]
        egc = jnp.exp(gcol)
        qh = q_ref[0, :, sl]
        kh = k_ref[0, :, sl]
        vh = v_ref[0, :, sl]
        kb = kh * beta
        khb = kh.astype(BF16)
        m = jnp.where(strict, _dot_nt(kb.astype(BF16), khb) * decay, 0.0)
        attn = _dot_nt(qh.astype(BF16), khb) * decay
        tinv = _unit_lower_inverse(m, row, col).astype(BF16)
        s_old = s_ref[h]
        sb = s_old.astype(BF16)
        w = vh * beta - _dot((kb * egc).astype(BF16), sb)
        u = _dot(tinv, w.astype(BF16))
        ub = u.astype(BF16)
        o = _dot((qh * egc).astype(BF16), sb) + _dot(attn.astype(BF16), ub)
        glast = gc[c - 1:c, h:h + 1]
        kdec = kh * jnp.exp(glast - gcol)
        s_ref[h] = s_old * jnp.exp(glast) + _dot(kdec.T.astype(BF16), ub)
        y_ref[0, :, sl] = _gdn_out(o, gate_ref[0, :, sl], onw).astype(y_ref.dtype)

    @pl.when(i == pl.num_programs(1) - 1)
    def _():
        sl_ref[0] = s_ref[...]


def gdn_chunked(q, k, v, gb, proj, o_norm_w, s0):
    b, t, _ = q.shape
    c = GDN_CHUNK
    blk = lambda bi, i: (bi, i, 0)
    return pl.pallas_call(
        _gdn_chunk_kernel,
        grid=(b, t // c),
        in_specs=[pl.BlockSpec((1, c, GDN_W), blk), pl.BlockSpec((1, c, GDN_W), blk),
                  pl.BlockSpec((1, c, GDN_W), blk), pl.BlockSpec((1, c, 128), blk),
                  pl.BlockSpec((1, c, GDN_W), lambda bi, i: (bi, i, 3)),
                  pl.BlockSpec((1, GDN_DH), lambda bi, i: (0, 0)),
                  pl.BlockSpec((1, GDN_HEADS, GDN_DH, GDN_DH), lambda bi, i: (bi, 0, 0, 0))],
        out_specs=[pl.BlockSpec((1, c, GDN_W), blk),
                   pl.BlockSpec((1, GDN_HEADS, GDN_DH, GDN_DH), lambda bi, i: (bi, 0, 0, 0))],
        out_shape=[jax.ShapeDtypeStruct((b, t, GDN_W), BF16),
                   jax.ShapeDtypeStruct((b, GDN_HEADS, GDN_DH, GDN_DH), F32)],
        scratch_shapes=[pltpu.VMEM((GDN_HEADS, GDN_DH, GDN_DH), F32)],
        compiler_params=_cparams(("parallel", "arbitrary")),
        name="gdn_chunked",
    )(q, k, v, gb, proj, o_norm_w.reshape(1, GDN_DH), s0)


def _gdn_steps_kernel(q_ref, k_ref, v_ref, gb_ref, gate_ref, onw_ref, s0_ref, y_ref, sl_ref, *, steps):
    gb = gb_ref[0]
    onw = onw_ref[...]
    zpad = jnp.zeros((GDN_DH - 8, GDN_DH), F32)
    for h in range(GDN_HEADS):
        sl = slice(h * GDN_DH, (h + 1) * GDN_DH)
        kt = jnp.concatenate([k_ref[0, :, sl], zpad], axis=0).T
        qt = jnp.concatenate([q_ref[0, :, sl], zpad], axis=0).T
        vh = v_ref[0, :, sl]
        s = s0_ref[0, h]
        outs = []
        for t in range(steps):
            kcol = kt[:, t:t + 1]
            qcol = qt[:, t:t + 1]
            s = s * jnp.exp(gb[t:t + 1, h:h + 1])
            u = gb[t:t + 1, GDN_HEADS + h:GDN_HEADS + h + 1] * (
                vh[t:t + 1, :] - jnp.sum(s * kcol, axis=0, keepdims=True))
            s = s + kcol * u
            outs.append(jnp.sum(s * qcol, axis=0, keepdims=True))
        outs.append(jnp.zeros((8 - steps, GDN_DH), F32))
        o = jnp.concatenate(outs, axis=0)
        y_ref[0, :, sl] = _gdn_out(o, gate_ref[0, :, sl], onw).astype(y_ref.dtype)
        sl_ref[0, h] = s


def gdn_steps(q, k, v, gb, proj, o_norm_w, s0, steps):
    b = q.shape[0]
    blk = lambda bi: (bi, 0, 0)
    kern = functools.partial(_gdn_steps_kernel, steps=steps)
    return pl.pallas_call(
        kern,
        grid=(b,),
        in_specs=[pl.BlockSpec((1, 8, GDN_W), blk), pl.BlockSpec((1, 8, GDN_W), blk),
                  pl.BlockSpec((1, 8, GDN_W), blk), pl.BlockSpec((1, 8, 128), blk),
                  pl.BlockSpec((1, 8, GDN_W), lambda bi: (bi, 0, 3)),
                  pl.BlockSpec((1, GDN_DH), lambda bi: (0, 0)),
                  pl.BlockSpec((1, GDN_HEADS, GDN_DH, GDN_DH), lambda bi: (bi, 0, 0, 0))],
        out_specs=[pl.BlockSpec((1, 8, GDN_W), blk),
                   pl.BlockSpec((1, GDN_HEADS, GDN_DH, GDN_DH), lambda bi: (bi, 0, 0, 0))],
        out_shape=[jax.ShapeDtypeStruct((b, 8, GDN_W), BF16),
                   jax.ShapeDtypeStruct((b, GDN_HEADS, GDN_DH, GDN_DH), F32)],
        compiler_params=_cparams(("parallel",)),
        name="gdn_steps",
    )(q, k, v, gb, proj, o_norm_w.reshape(1, GDN_DH), s0)


def gdn_mixer(x, gain_pre, gain_post, conv_buf, s0, wts, tm, tt):
    w_in, conv_w, a_log, dt_bias, o_norm_w, w_out = wts
    b, t, d = x.shape
    x2 = x.reshape(b * t, d)
    proj = norm_matmul(x2, gain_pre, w_in, tm, 512).reshape(b, t, GDN_PROJ)
    if t % GDN_CHUNK == 0:
        q, k, v, gb, nbuf = gdn_prep(proj, conv_buf, conv_w, a_log, dt_bias, t, tt)
        y, s_new = gdn_chunked(q, k, v, gb, proj, o_norm_w, s0)
    else:
        assert t <= 8
        proj = jnp.pad(proj, ((0, 0), (0, 8 - t), (0, 0)))
        q, k, v, gb, nbuf = gdn_prep(proj, conv_buf, conv_w, a_log, dt_bias, t, 8)
        y, s_new = gdn_steps(q, k, v, gb, proj, o_norm_w, s0, t)
        y = y[:, :t]
    out = matmul_norm_res(y.reshape(b * t, d), w_out, gain_post, x2, tm)
    return out.reshape(b, t, d), nbuf, s_new


NSA_PROJ = 3072
NSA_Q_COLS = NSA_HEADS * NSA_DH
NSA_GATE_COL = NSA_Q_COLS + 6 * NSA_KVW
NSA_QB = 128
NSA_TK = 256


def _online(s, mask, v, m, l, acc):
    s = jnp.where(mask, s, NEG)
    m_new = jnp.maximum(m, jnp.max(s, axis=1, keepdims=True))
    alpha = jnp.exp(m - m_new)
    p = jnp.where(mask, jnp.exp(s - m_new), 0.0)
    l = alpha * l + jnp.sum(p, axis=1, keepdims=True)
    acc = alpha * acc + _dot(p.astype(BF16), v)
    return m_new, l, acc


def _masked_softmax(s, mask):
    s = jnp.where(mask, s, NEG)
    m = jnp.max(s, axis=1, keepdims=True)
    p = jnp.where(mask, jnp.exp(s - m), 0.0)
    return p / jnp.maximum(jnp.sum(p, axis=1, keepdims=True), 1e-30)


def _select_top(score, lane_f, n_sel, n_lanes):
    sel = jnp.zeros(score.shape, F32)
    sc = score
    for _ in range(n_sel):
        mx = jnp.max(sc, axis=1, keepdims=True)
        idx = jnp.min(jnp.where(sc == mx, lane_f, float(n_lanes)), axis=1, keepdims=True)
        pick = lane_f == idx
        sel = jnp.where(pick, 1.0, sel)
        sc = jnp.where(pick, -3e38, sc)
    return sel


def _nsa_compress_kernel(xk_ref, xv_ref, wk_ref, wv_ref, pe_ref, ck_ref, cv_ref, acc_ref):
    p = pl.program_id(1)

    @pl.when(p == 0)
    def _():
        acc_ref[...] = jnp.zeros_like(acc_ref)

    acc_ref[0] += _dot((xk_ref[0] + pe_ref[0, 0]).astype(BF16), wk_ref[0, 0])
    acc_ref[1] += _dot((xv_ref[0] + pe_ref[1, 0]).astype(BF16), wv_ref[0, 0])

    @pl.when(p == pl.num_programs(1) - 1)
    def _():
        ck_ref[0] = acc_ref[0]
        cv_ref[0] = acc_ref[1]


def nsa_compress(projb, wbd, pe4):
    b, nb, _ = projb.shape
    per = NSA_PROJ // NSA_KVW
    kcol = NSA_Q_COLS // NSA_KVW
    return pl.pallas_call(
        _nsa_compress_kernel,
        grid=(b, NSA_BLOCK),
        in_specs=[pl.BlockSpec((1, nb, NSA_KVW), lambda bi, p: (bi, 0, p * per + kcol)),
                  pl.BlockSpec((1, nb, NSA_KVW), lambda bi, p: (bi, 0, p * per + kcol + 1)),
                  pl.BlockSpec((1, 1, NSA_KVW, NSA_KVW), lambda bi, p: (0, p, 0, 0)),
                  pl.BlockSpec((1, 1, NSA_KVW, NSA_KVW), lambda bi, p: (1, p, 0, 0)),
                  pl.BlockSpec((2, 1, 1, NSA_KVW), lambda bi, p: (0, p, 0, 0))],
        out_specs=[pl.BlockSpec((1, nb, NSA_KVW), lambda bi, p: (bi, 0, 0)),
                   pl.BlockSpec((1, nb, NSA_KVW), lambda bi, p: (bi, 0, 0))],
        out_shape=[jax.ShapeDtypeStruct((b, nb, NSA_KVW), F32)] * 2,
        scratch_shapes=[pltpu.VMEM((2, nb, NSA_KVW), F32)],
        compiler_params=_cparams(("parallel", "arbitrary")),
        name="nsa_compress",
    )(projb, projb, wbd, wbd, pe4)


def _nsa_prompt_kernel(q_ref, gt_ref, ck_ref, cv_ref, kv_ref, o_ref, *, nb, n_sel):
    i = pl.program_id(2)
    qb, dh, hpg = NSA_QB, NSA_DH, NSA_HPG
    q = q_ref[0] * NSA_SCALE
    qs = jnp.concatenate([q[:, h * dh:(h + 1) * dh] for h in range(hpg)], axis=0).astype(BF16)

    tq = lax.broadcasted_iota(jnp.int32, (qb, nb), 0) + i * qb
    nblk = lax.broadcasted_iota(jnp.int32, (qb, nb), 1)
    ok_c = (nblk + 1) * NSA_BLOCK - 1 <= tq
    s = _dot_nt(qs, ck_ref[0, 0])
    imp = jnp.zeros((qb, nb), F32)
    ps = []
    for h in range(hpg):
        p = _masked_softmax(s[h * qb:(h + 1) * qb], ok_c)
        imp = imp + p
        ps.append(p)
    o_cmp = _dot(jnp.concatenate(ps, axis=0).astype(BF16), cv_ref[0, 0])

    cur = tq // NSA_BLOCK
    forced = (nblk == 0) | (nblk == cur) | (nblk == cur - 1)
    score = jnp.where(nblk > cur, -1.0, jnp.where(forced, NSA_FORCE, imp))
    sel = _select_top(score, nblk.astype(F32), n_sel, nb).astype(BF16)

    def init():
        return (tuple(jnp.full((qb, 1), NEG, F32) for _ in range(hpg)),
                tuple(jnp.zeros((qb, 1), F32) for _ in range(hpg)),
                tuple(jnp.zeros((qb, dh), F32) for _ in range(hpg)))

    def attend(k, v, mask, carry):
        ms, ls, accs = carry
        sc = _dot_nt(qs, k)
        out = [_online(sc[h * qb:(h + 1) * qb], mask, v, ms[h], ls[h], accs[h]) for h in range(hpg)]
        return (tuple(o[0] for o in out), tuple(o[1] for o in out), tuple(o[2] for o in out))

    def finish(carry):
        _, ls, accs = carry
        return [accs[h] / jnp.maximum(ls[h], 1e-30) for h in range(hpg)]

    tk = NSA_TK
    tq_k = lax.broadcasted_iota(jnp.int32, (qb, tk), 0) + i * qb
    kcol = lax.broadcasted_iota(jnp.int32, (qb, tk), 1)
    e_row = lax.broadcasted_iota(jnp.int32, (nb, tk), 0)
    e_col = lax.broadcasted_iota(jnp.int32, (nb, tk), 1) // NSA_BLOCK

    def slc_body(kt, carry):
        k = kv_ref[0, 0, pl.ds(kt * tk, tk), 0:dh]
        v = kv_ref[0, 0, pl.ds(kt * tk, tk), dh:2 * dh]
        expand = (e_row == e_col + kt * (tk // NSA_BLOCK)).astype(BF16)
        mask = (_dot(sel, expand) > 0.5) & (kcol + kt * tk <= tq_k)
        return attend(k, v, mask, carry)

    n_tiles = (i * qb + qb + tk - 1) // tk
    o_slc = finish(lax.fori_loop(0, n_tiles, slc_body, init()))

    tq_w = lax.broadcasted_iota(jnp.int32, (qb, qb), 0) + i * qb
    wcol = lax.broadcasted_iota(jnp.int32, (qb, qb), 1)
    n_wt = NSA_WINDOW // qb + 1

    def win_body(jj, carry):
        kt = i - (n_wt - 1) + jj
        k = kv_ref[0, 0, pl.ds(kt * qb, qb), 2 * dh:3 * dh]
        v = kv_ref[0, 0, pl.ds(kt * qb, qb), 3 * dh:4 * dh]
        dist = tq_w - (wcol + kt * qb)
        return attend(k, v, (dist >= 0) & (dist < NSA_WINDOW), carry)

    o_win = finish(lax.fori_loop(jnp.maximum(0, n_wt - 1 - i), n_wt, win_body, init()))

    gt = _sigmoid(gt_ref[0, 0])
    outs = []
    for h in range(hpg):
        outs.append(gt[:, 3 * h:3 * h + 1] * o_cmp[h * qb:(h + 1) * qb]
                    + gt[:, 3 * h + 1:3 * h + 2] * o_slc[h] + gt[:, 3 * h + 2:3 * h + 3] * o_win[h])
    o_ref[0] = jnp.concatenate(outs, axis=1).astype(o_ref.dtype)


def nsa_prompt_attn(proj3, gt, ckg, cvg, kvg):
    b, t, _ = proj3.shape
    g = NSA_GROUPS
    nb = ckg.shape[2]
    kern = functools.partial(_nsa_prompt_kernel, nb=nb, n_sel=min(NSA_TOP_N, nb))
    return pl.pallas_call(
        kern,
        grid=(b, g, t // NSA_QB),
        in_specs=[pl.BlockSpec((1, NSA_QB, NSA_KVW), lambda bi, gi, i: (bi, i, gi)),
                  pl.BlockSpec((1, 1, NSA_QB, 12), lambda bi, gi, i: (bi, gi, i, 0)),
                  pl.BlockSpec((1, 1, nb, NSA_DH), lambda bi, gi, i: (bi, gi, 0, 0)),
                  pl.BlockSpec((1, 1, nb, NSA_DH), lambda bi, gi, i: (bi, gi, 0, 0)),
                  pl.BlockSpec((1, 1, t, NSA_KVW), lambda bi, gi, i: (bi, gi, 0, 0))],
        out_specs=pl.BlockSpec((1, NSA_QB, NSA_KVW), lambda bi, gi, i: (bi, i, gi)),
        out_shape=jax.ShapeDtypeStruct((b, t, NSA_Q_COLS), BF16),
        compiler_params=_cparams(("parallel", "parallel", "arbitrary")),
        name="nsa_prompt_attn",
    )(proj3, gt, ckg, cvg, kvg)


def nsa_prompt(x, gain_pre, gain_post, wts, tm):
    w_in, wbd, pe4, w_out = wts
    b, t, d = x.shape
    x2 = x.reshape(b * t, d)
    proj = norm_matmul(x2, gain_pre, w_in, tm, 512)
    proj3 = proj.reshape(b, t, NSA_PROJ)
    nb = t // NSA_BLOCK
    ck, cv = nsa_compress(proj.reshape(b, nb, NSA_BLOCK * NSA_PROJ), wbd, pe4)
    to_g = lambda c: c.reshape(b, nb, NSA_GROUPS, NSA_DH).transpose(0, 2, 1, 3).astype(BF16)
    c0 = NSA_Q_COLS
    kv_rows = proj3[:, :, c0:c0 + 4 * NSA_KVW].reshape(b, t, 4, NSA_GROUPS, NSA_DH)
    wb = min(NSA_WINDOW, t)
    win_rows = proj3[:, t - wb:, c0 + 4 * NSA_KVW:c0 + 6 * NSA_KVW].reshape(b, wb, 2, NSA_GROUPS, NSA_DH)
    kvg = proj3[:, :, c0 + 2 * NSA_KVW:c0 + 6 * NSA_KVW].reshape(b, t, 4, NSA_GROUPS, NSA_DH)
    kvg = kvg.transpose(0, 3, 1, 2, 4).reshape(b, NSA_GROUPS, t, 4 * NSA_DH).astype(BF16)
    gt = proj3[:, :, NSA_GATE_COL:NSA_GATE_COL + 3 * NSA_HEADS].reshape(b, t, NSA_GROUPS, 3 * NSA_HPG)
    o = nsa_prompt_attn(proj3, gt.transpose(0, 2, 1, 3), to_g(ck), to_g(cv), kvg)
    out = matmul_norm_res(o.reshape(b * t, NSA_Q_COLS), w_out, gain_post, x2, tm)
    return out.reshape(b, t, d), kv_rows, win_rows


def nsa_weights(w_in, cmp_pe, cmp_w, w_out):
    w_in_p = jnp.pad(w_in, ((0, 0), (0, NSA_PROJ - w_in.shape[1]))).astype(BF16)
    eye = jnp.eye(NSA_GROUPS, dtype=F32)
    wbd = jnp.einsum('gh,cpde->cpgdhe', eye, cmp_w).reshape(2, NSA_BLOCK, NSA_KVW, NSA_KVW).astype(BF16)
    pe4 = jnp.tile(cmp_pe.transpose(1, 0, 2)[:, :, None, :], (1, 1, 1, NSA_GROUPS))
    return w_in_p, wbd, pe4, w_out.astype(BF16)


PAGES_PER_STEP = 8


def _nsa_compress_paged_kernel(pt_ref, *refs, nb, n_steps):
    pages = refs[:PAGES_PER_STEP]
    wk_ref, wv_ref, pe_ref, ck_ref, cv_ref, stg_ref, acc_ref = refs[PAGES_PER_STEP:]
    bi = pl.program_id(0)
    j = pl.program_id(1)
    page_rows = pages[0].shape[1]
    slot = bi % 2
    for r in range(PAGES_PER_STEP):
        for c in range(4):
            stg_ref[slot * 4 + c, pl.ds((j * PAGES_PER_STEP + r) * page_rows, page_rows), :] = (
                pages[r][0, :, c * 128:(c + 1) * 128])

    @pl.when(bi > 0)
    def _():
        @pl.when(j == 0)
        def _():
            acc_ref[...] = jnp.zeros_like(acc_ref)

        prev = 1 - slot
        pps = wk_ref.shape[1]
        for r in range(pps):
            p = j * pps + r
            rows_p = pl.ds(p, nb, stride=NSA_BLOCK)
            xk = jnp.concatenate([stg_ref[prev * 4, rows_p, :], stg_ref[prev * 4 + 1, rows_p, :]], axis=1)
            xv = jnp.concatenate([stg_ref[prev * 4 + 2, rows_p, :], stg_ref[prev * 4 + 3, rows_p, :]], axis=1)
            acc_ref[0] += _dot((xk + pe_ref[0, r]).astype(BF16), wk_ref[0, r])
            acc_ref[1] += _dot((xv + pe_ref[1, r]).astype(BF16), wv_ref[0, r])

        @pl.when(j == n_steps - 1)
        def _():
            ck_ref[0] = acc_ref[0]
            cv_ref[0] = acc_ref[1]


def nsa_compress_paged(cache2, page_table, wbd, pe4):
    b, n_pages = page_table.shape
    page_rows = cache2.shape[1]
    past = n_pages * page_rows
    nb = past // NSA_BLOCK
    n_steps = n_pages // PAGES_PER_STEP
    pps = NSA_BLOCK // n_steps
    assert n_pages % PAGES_PER_STEP == 0 and NSA_BLOCK % n_steps == 0 and page_rows % NSA_BLOCK == 0
    pt = page_table.reshape(-1)

    def page_map(r):
        return lambda bi, j, pt_ref: (pt_ref[jnp.minimum(bi, b - 1) * n_pages + j * PAGES_PER_STEP + r], 0, 0)

    out_map = lambda bi, j, pt_ref: (jnp.maximum(bi - 1, 0), 0, 0)
    kern = functools.partial(_nsa_compress_paged_kernel, nb=nb, n_steps=n_steps)
    grid_spec = pltpu.PrefetchScalarGridSpec(
        num_scalar_prefetch=1,
        grid=(b + 1, n_steps),
        in_specs=[pl.BlockSpec((1, page_rows, 2 * NSA_KVW), page_map(r)) for r in range(PAGES_PER_STEP)]
        + [pl.BlockSpec((1, pps, NSA_KVW, NSA_KVW), lambda bi, j, pt_ref: (0, j, 0, 0)),
           pl.BlockSpec((1, pps, NSA_KVW, NSA_KVW), lambda bi, j, pt_ref: (1, j, 0, 0)),
           pl.BlockSpec((2, pps, 1, NSA_KVW), lambda bi, j, pt_ref: (0, j, 0, 0))],
        out_specs=[pl.BlockSpec((1, nb, NSA_KVW), out_map), pl.BlockSpec((1, nb, NSA_KVW), out_map)],
        scratch_shapes=[pltpu.VMEM((8, past, 128), F32), pltpu.VMEM((2, nb, NSA_KVW), F32)],
    )
    return pl.pallas_call(
        kern,
        grid_spec=grid_spec,
        out_shape=[jax.ShapeDtypeStruct((b, nb, NSA_KVW), F32)] * 2,
        compiler_params=_cparams(("arbitrary", "arbitrary")),
        name="nsa_compress_paged",
    )(pt, *([cache2] * PAGES_PER_STEP), wbd, wbd, pe4)


def _diag_blocks(x, row_g):
    out = jnp.zeros((x.shape[0], NSA_DH), F32)
    for g in range(NSA_GROUPS):
        out = out + jnp.where(row_g == g, x[:, g * NSA_DH:(g + 1) * NSA_DH], 0.0)
    return out


def _nsa_sample_kernel(pt_ref, *refs, nb, ts, past, n_steps):
    pages = refs[:PAGES_PER_STEP]
    (q_ref, gt_ref, ck_ref, cv_ref, new_ref, win_ref, o_ref,
     sel_ref, ocmp_ref, m_ref, l_ref, acc_ref) = refs[PAGES_PER_STEP:]
    j = pl.program_id(1)
    rows = NSA_HEADS * ts
    page_rows = pages[0].shape[1]
    bpp = page_rows // NSA_BLOCK
    qbd = (q_ref[0] * NSA_SCALE).astype(BF16)
    rid = lax.broadcasted_iota(jnp.int32, (rows, 1), 0)
    row_t = rid % ts
    row_g = (rid // ts) % NSA_GROUPS

    @pl.when(j == 0)
    def _():
        nblk = lax.broadcasted_iota(jnp.int32, (rows, nb), 1)
        qpos = past + row_t
        ok_c = (nblk + 1) * NSA_BLOCK - 1 <= qpos
        p = _masked_softmax(_dot_nt(qbd, ck_ref[0].astype(BF16)), ok_c)
        ocmp_ref[...] = _diag_blocks(_dot(p.astype(BF16), cv_ref[0].astype(BF16)), row_g)
        gt_rows = NSA_GROUPS * ts
        imp = p[0:gt_rows]
        for h in range(1, NSA_HPG):
            imp = imp + p[h * gt_rows:(h + 1) * gt_rows]
        nb1 = lax.broadcasted_iota(jnp.int32, (gt_rows, nb), 1)
        cur = (past + lax.broadcasted_iota(jnp.int32, (gt_rows, nb), 0) % ts) // NSA_BLOCK
        forced = (nb1 == 0) | (nb1 == cur) | (nb1 == cur - 1)
        score = jnp.where(nb1 > cur, -1.0, jnp.where(forced, NSA_FORCE, imp))
        sel = _select_top(score, nb1.astype(F32), NSA_TOP_N - 1, nb)
        sel_ref[...] = jnp.concatenate([sel] * NSA_HPG, axis=0)
        m_ref[...] = jnp.full(m_ref.shape, NEG, F32)
        l_ref[...] = jnp.zeros_like(l_ref)
        acc_ref[...] = jnp.zeros_like(acc_ref)

    sel = sel_ref[...]
    lane = lax.broadcasted_iota(jnp.int32, (rows, nb), 1)
    kblk = lax.broadcasted_iota(jnp.int32, (rows, page_rows), 1) // NSA_BLOCK
    m, l, acc = m_ref[...], l_ref[...], acc_ref[...]
    for r in range(PAGES_PER_STEP):
        blk0 = (j * PAGES_PER_STEP + r) * bpp
        mask = jnp.zeros((rows, page_rows), jnp.bool_)
        for c in range(bpp):
            on = jnp.sum(jnp.where(lane == blk0 + c, sel, 0.0), axis=1, keepdims=True) > 0.5
            mask = mask | ((kblk == c) & on)
        k = pages[r][0, :, 0:NSA_KVW].astype(BF16)
        v = pages[r][0, :, NSA_KVW:2 * NSA_KVW].astype(BF16)
        m, l, acc = _online(_dot_nt(qbd, k), mask, v, m, l, acc)
    m_ref[...], l_ref[...], acc_ref[...] = m, l, acc

    @pl.when(j == n_steps - 1)
    def _():
        new = new_ref[0]
        kidx = lax.broadcasted_iota(jnp.int32, (rows, 8), 1)
        new_ok = (kidx <= row_t) & (kidx < ts)
        ms, ls, accs = _online(_dot_nt(qbd, new[:, 0:NSA_KVW].astype(BF16)), new_ok,
                               new[:, NSA_KVW:2 * NSA_KVW].astype(BF16), m, l, acc)
        o_slc = _diag_blocks(accs, row_g) / jnp.maximum(ls, 1e-30)
        win = win_ref[0]
        wl = win.shape[0]
        wr = lax.broadcasted_iota(jnp.int32, (rows, wl), 1)
        dist = row_t + wl - wr
        mw = jnp.full((rows, 1), NEG, F32)
        lw = jnp.zeros((rows, 1), F32)
        aw = jnp.zeros((rows, NSA_KVW), F32)
        mw, lw, aw = _online(_dot_nt(qbd, win[:, 0:NSA_KVW].astype(BF16)), (dist >= 0) & (dist < NSA_WINDOW),
                             win[:, NSA_KVW:2 * NSA_KVW].astype(BF16), mw, lw, aw)
        mw, lw, aw = _online(_dot_nt(qbd, new[:, 2 * NSA_KVW:3 * NSA_KVW].astype(BF16)), new_ok,
                             new[:, 3 * NSA_KVW:4 * NSA_KVW].astype(BF16), mw, lw, aw)
        o_win = _diag_blocks(aw, row_g) / jnp.maximum(lw, 1e-30)
        gt = _sigmoid(gt_ref[0])
        o_ref[0] = gt[:, 0:1] * ocmp_ref[...] + gt[:, 1:2] * o_slc + gt[:, 2:3] * o_win


def nsa_sample_attn(cache2, page_table, qbd, gt, ck, cv, new_kv, win2, ts):
    b, n_pages = page_table.shape
    page_rows = cache2.shape[1]
    past = n_pages * page_rows
    nb = past // NSA_BLOCK
    n_steps = n_pages // PAGES_PER_STEP
    rows = NSA_HEADS * ts
    assert n_pages % PAGES_PER_STEP == 0 and past % NSA_BLOCK == 0 and ts <= 8 and nb >= NSA_TOP_N
    pt = page_table.reshape(-1)

    def page_map(r):
        return lambda bi, j, pt_ref: (pt_ref[bi * n_pages + j * PAGES_PER_STEP + r], 0, 1)

    per_b = lambda bi, j, pt_ref: (bi, 0, 0)
    kern = functools.partial(_nsa_sample_kernel, nb=nb, ts=ts, past=past, n_steps=n_steps)
    grid_spec = pltpu.PrefetchScalarGridSpec(
        num_scalar_prefetch=1,
        grid=(b, n_steps),
        in_specs=[pl.BlockSpec((1, page_rows, 2 * NSA_KVW), page_map(r)) for r in range(PAGES_PER_STEP)]
        + [pl.BlockSpec((1, rows, NSA_KVW), per_b), pl.BlockSpec((1, rows, 3), per_b),
           pl.BlockSpec((1, nb, NSA_KVW), per_b), pl.BlockSpec((1, nb, NSA_KVW), per_b),
           pl.BlockSpec((1, 8, 4 * NSA_KVW), per_b),
           pl.BlockSpec((1, win2.shape[1], 2 * NSA_KVW), per_b)],
        out_specs=pl.BlockSpec((1, rows, NSA_DH), per_b),
        scratch_shapes=[pltpu.VMEM((rows, nb), F32), pltpu.VMEM((rows, NSA_DH), F32),
                        pltpu.VMEM((rows, 1), F32), pltpu.VMEM((rows, 1), F32),
                        pltpu.VMEM((rows, NSA_KVW), F32)],
    )
    return pl.pallas_call(
        kern,
        grid_spec=grid_spec,
        out_shape=jax.ShapeDtypeStruct((b, rows, NSA_DH), F32),
        compiler_params=_cparams(("parallel", "arbitrary")),
        name="nsa_sample_attn",
    )(pt, *([cache2] * PAGES_PER_STEP), qbd, gt, ck, cv, new_kv, win2)


def _win_shift_kernel(win_ref, new_ref, o_ref, *, ts):
    wl = win_ref.shape[1]
    o_ref[0, 0:wl - ts, :] = win_ref[0, ts:wl, :]
    o_ref[0, wl - ts:wl, :] = new_ref[0, 0:ts, :]


def win_shift(win2, new_win, ts):
    b, wl, w = win2.shape
    return pl.pallas_call(
        functools.partial(_win_shift_kernel, ts=ts),
        grid=(b,),
        in_specs=[pl.BlockSpec((1, wl, w), lambda bi: (bi, 0, 0)),
                  pl.BlockSpec((1, 8, w), lambda bi: (bi, 0, 0))],
        out_specs=pl.BlockSpec((1, wl, w), lambda bi: (bi, 0, 0)),
        out_shape=jax.ShapeDtypeStruct((b, wl, w), F32),
        compiler_params=_cparams(("parallel",)),
        name="win_shift",
    )(win2, new_win)


def nsa_sample(x, gain_pre, gain_post, cache_kv, cache_win, page_table, wts):
    w_in, wbd, pe4, w_out = wts
    b, ts, d = x.shape
    g, hpg, dh = NSA_GROUPS, NSA_HPG, NSA_DH
    x2 = x.reshape(b * ts, d)
    proj = norm_matmul(x2, gain_pre, w_in, b * ts, 512).reshape(b, ts, NSA_PROJ)
    c0 = NSA_Q_COLS
    kv_rows = proj[:, :, c0:c0 + 4 * NSA_KVW].reshape(b, ts, 4, g, dh)
    cache2 = cache_kv.reshape(cache_kv.shape[0], cache_kv.shape[1], 4 * NSA_KVW)
    wl = cache_win.shape[1]
    win2 = cache_win.reshape(b, wl, 2 * NSA_KVW)
    new_kv = jnp.pad(proj[:, :, c0 + 2 * NSA_KVW:c0 + 6 * NSA_KVW], ((0, 0), (0, 8 - ts), (0, 0)))
    win_rows = win_shift(win2, new_kv[:, :, 2 * NSA_KVW:], ts).reshape(b, wl, 2, g, dh)
    ck, cv = nsa_compress_paged(cache2, page_table, wbd, pe4)
    q = proj[:, :, :c0].reshape(b, ts, g, hpg, dh).transpose(0, 3, 2, 1, 4)
    qbd = jnp.einsum('bhgtd,gk->bhgtkd', q, jnp.eye(g, dtype=F32)).reshape(b, hpg * g * ts, g * dh)
    gt = proj[:, :, NSA_GATE_COL:NSA_GATE_COL + 3 * NSA_HEADS].reshape(b, ts, g, hpg, 3)
    gt = gt.transpose(0, 3, 2, 1, 4).reshape(b, hpg * g * ts, 3)
    o = nsa_sample_attn(cache2, page_table, qbd, gt, ck, cv, new_kv, win2, ts)
    o = o.reshape(b, hpg, g, ts, dh).transpose(0, 3, 2, 1, 4).reshape(b * ts, NSA_Q_COLS).astype(BF16)
    out = matmul_norm_res(o, w_out, gain_post, x2, b * ts)
    return out.reshape(b, ts, d), kv_rows, win_rows


def kernel(x_prompt, x_sample, state_l0_lru_conv, state_l0_lru_h, cache_l1_kv, cache_l1_win, state_l2_gdn_conv, state_l2_gdn_s, state_l3_lru_conv, state_l3_lru_h, page_table, norm_gains, ffn_w_gate_up, ffn_w_down, l0_w_in, l0_conv_w, l0_conv_b, l0_gate_w, l0_gate_b, l0_lam, l0_w_out, l1_w_in, l1_cmp_pe, l1_cmp_w, l1_w_out, l2_w_in, l2_conv_w, l2_a_log, l2_dt_bias, l2_o_norm_w, l2_w_out, l3_w_in, l3_conv_w, l3_conv_b, l3_gate_w, l3_gate_b, l3_lam, l3_w_out):
    bp, tp, d = x_prompt.shape
    bs, ts, _ = x_sample.shape
    tm_p = 1024
    tm_s = bs * ts
    lru_w = {
        0: (l0_w_in.astype(BF16), l0_conv_w, l0_conv_b, l0_gate_w.astype(BF16), l0_gate_b, l0_lam,
            l0_w_out.astype(BF16)),
        3: (l3_w_in.astype(BF16), l3_conv_w, l3_conv_b, l3_gate_w.astype(BF16), l3_gate_b, l3_lam,
            l3_w_out.astype(BF16)),
    }
    lru_s = {0: (state_l0_lru_conv, state_l0_lru_h), 3: (state_l3_lru_conv, state_l3_lru_h)}
    nsa_w = nsa_weights(l1_w_in, l1_cmp_pe, l1_cmp_w, l1_w_out)
    gdn_w = (jnp.pad(l2_w_in, ((0, 0), (0, GDN_PROJ - l2_w_in.shape[1]))).astype(BF16), l2_conv_w, l2_a_log,
             l2_dt_bias, l2_o_norm_w, l2_w_out.astype(BF16))
    xp, xs = x_prompt, x_sample
    new = {}
    for layer in range(4):
        g = norm_gains[layer]
        kind = layer % 3
        if kind == 0:
            conv_s, h_s = lru_s[layer]
            xp, cp, hp = lru_mixer(xp, g[0], g[1], jnp.zeros((bp, CONV_W - 1, d), F32), jnp.zeros((bp, d), F32),
                                   lru_w[layer], 512, 256)
            xs, cs, hs = lru_mixer(xs, g[0], g[1], conv_s, h_s, lru_w[layer], tm_s, 256)
            new[layer] = (cp, hp, cs, hs)
        elif kind == 1:
            xp, kv_p, win_p = nsa_prompt(xp, g[0], g[1], nsa_w, 512)
            xs, kv_s, win_s = nsa_sample(xs, g[0], g[1], cache_l1_kv, cache_l1_win, page_table, nsa_w)
            new[layer] = (kv_p, win_p, kv_s, win_s)
        else:
            xp, cp, sp = gdn_mixer(xp, g[0], g[1], jnp.zeros((bp, CONV_W - 1, 3 * GDN_W), F32),
                                   jnp.zeros((bp, GDN_HEADS, GDN_DH, GDN_DH), F32), gdn_w, 512, 256)
            xs, cs, ss = gdn_mixer(xs, g[0], g[1], state_l2_gdn_conv, state_l2_gdn_s, gdn_w, tm_s, 8)
            new[layer] = (cp, sp, cs, ss)
        wgu = ffn_w_gate_up[layer].astype(BF16)
        wd = ffn_w_down[layer].astype(BF16)
        xp = ffn(xp.reshape(bp * tp, d), g[2], g[3], wgu, wd, tm_p).reshape(bp, tp, d)
        xs = ffn(xs.reshape(bs * ts, d), g[2], g[3], wgu, wd, tm_s).reshape(bs, ts, d)
    return (xp, xs) + new[0] + new[1] + new[2] + new[3]
```

```python
import functools

import jax
import jax.numpy as jnp
from jax import lax
from jax.experimental import pallas as pl
from jax.experimental.pallas import tpu as pltpu

F32 = jnp.float32
BF16 = jnp.bfloat16

D_MODEL = 1024
D_FF = 2816
EPS = 1e-6
CONV_W = 4
LRU_BLOCKS = 4
LRU_BW = 256
LRU_C = 8.0
NSA_HEADS = 16
NSA_DH = 64
NSA_GROUPS = 4
NSA_HPG = 4
NSA_KVW = 256
NSA_BLOCK = 64
NSA_TOP_N = 16
NSA_WINDOW = 512
NSA_FORCE = 1e4
NSA_SCALE = NSA_DH ** -0.5
GDN_HEADS = 8
GDN_DH = 128
NEG = -1e30

VMEM_LIMIT = 52 * 1024 * 1024


def _cparams(sem):
    return pltpu.CompilerParams(dimension_semantics=sem, vmem_limit_bytes=VMEM_LIMIT)


def _dot(a, b):
    return jnp.dot(a, b, preferred_element_type=F32)


def _dot_nt(a, b):
    return lax.dot_general(a, b, (((1,), (1,)), ((), ())), preferred_element_type=F32)


def _sigmoid(x):
    return 1.0 / (1.0 + jnp.exp(-x))


def _silu(x):
    return x * _sigmoid(x)


def _softplus(x):
    return jnp.maximum(x, 0.0) + jnp.log(1.0 + jnp.exp(-jnp.abs(x)))


def _gelu_tanh(x):
    c = 0.7978845608028654
    return 0.5 * x * (1.0 + jnp.tanh(c * (x + 0.044715 * x * x * x)))


def _rms(x, g):
    return x * lax.rsqrt(jnp.mean(x * x, axis=-1, keepdims=True) + EPS) * g


def _norm_matmul_kernel(x_ref, g_ref, w_ref, o_ref, xn_ref):
    @pl.when(pl.program_id(1) == 0)
    def _():
        xn_ref[...] = _rms(x_ref[...], g_ref[...]).astype(BF16)

    o_ref[...] = _dot(xn_ref[...], w_ref[...]).astype(o_ref.dtype)


def norm_matmul(x, gain, w, tm, tn, out_dtype=F32):
    m, d = x.shape
    n = w.shape[1]
    return pl.pallas_call(
        _norm_matmul_kernel,
        grid=(m // tm, n // tn),
        in_specs=[pl.BlockSpec((tm, d), lambda i, j: (i, 0)),
                  pl.BlockSpec((1, d), lambda i, j: (0, 0)),
                  pl.BlockSpec((d, tn), lambda i, j: (0, j))],
        out_specs=pl.BlockSpec((tm, tn), lambda i, j: (i, j)),
        out_shape=jax.ShapeDtypeStruct((m, n), out_dtype),
        scratch_shapes=[pltpu.VMEM((tm, d), BF16)],
        compiler_params=_cparams(("parallel", "arbitrary")),
        name="norm_matmul",
    )(x, gain.reshape(1, d), w)


def _matmul_norm_res_kernel(y_ref, w_ref, g_ref, r_ref, o_ref):
    m = _dot(y_ref[...], w_ref[...])
    o_ref[...] = r_ref[...] + _rms(m, g_ref[...])


def matmul_norm_res(y, w, gain, res, tm):
    m, k = y.shape
    d = w.shape[1]
    return pl.pallas_call(
        _matmul_norm_res_kernel,
        grid=(m // tm,),
        in_specs=[pl.BlockSpec((tm, k), lambda i: (i, 0)),
                  pl.BlockSpec((k, d), lambda i: (0, 0)),
                  pl.BlockSpec((1, d), lambda i: (0, 0)),
                  pl.BlockSpec((tm, d), lambda i: (i, 0))],
        out_specs=pl.BlockSpec((tm, d), lambda i: (i, 0)),
        out_shape=jax.ShapeDtypeStruct((m, d), F32),
        compiler_params=_cparams(("parallel",)),
        name="matmul_norm_res",
    )(y, w, gain.reshape(1, d), res)


FFN_TF = 256


def _ffn_kernel(x_ref, gpre_ref, gpost_ref, wg_ref, wu_ref, wd_ref, o_ref, xn_ref, acc_ref):
    j = pl.program_id(1)

    @pl.when(j == 0)
    def _():
        xn_ref[...] = _rms(x_ref[...], gpre_ref[...]).astype(BF16)
        acc_ref[...] = jnp.zeros_like(acc_ref)

    xn = xn_ref[...]
    act = _silu(_dot(xn, wg_ref[...])) * _dot(xn, wu_ref[...])
    acc_ref[...] += _dot(act.astype(BF16), wd_ref[...])

    @pl.when(j == pl.num_programs(1) - 1)
    def _():
        o_ref[...] = x_ref[...] + _rms(acc_ref[...], gpost_ref[...])


def ffn(x, g_pre, g_post, w_gate_up, w_down, tm):
    m, d = x.shape
    nf = D_FF // FFN_TF
    return pl.pallas_call(
        _ffn_kernel,
        grid=(m // tm, nf),
        in_specs=[pl.BlockSpec((tm, d), lambda i, j: (i, 0)),
                  pl.BlockSpec((1, d), lambda i, j: (0, 0)),
                  pl.BlockSpec((1, d), lambda i, j: (0, 0)),
                  pl.BlockSpec((d, FFN_TF), lambda i, j: (0, j)),
                  pl.BlockSpec((d, FFN_TF), lambda i, j: (0, j + nf)),
                  pl.BlockSpec((FFN_TF, d), lambda i, j: (j, 0))],
        out_specs=pl.BlockSpec((tm, d), lambda i, j: (i, 0)),
        out_shape=jax.ShapeDtypeStruct((m, d), F32),
        scratch_shapes=[pltpu.VMEM((tm, d), BF16), pltpu.VMEM((tm, d), F32)],
        compiler_params=_cparams(("parallel", "arbitrary")),
        name="ffn",
    )(x, g_pre.reshape(1, d), g_post.reshape(1, d), w_gate_up, w_gate_up, w_down)


def _lru_kernel(xb_ref, yb_ref, buf_ref, h0_ref, cw_ref, cb_ref, gw_ref, gb_ref, lam_ref,
                y_ref, nbuf_ref, hl_ref, ext_ref, a_ref, u_ref, hc_ref, *, tt, t_last):
    i = pl.program_id(1)
    w = D_MODEL

    @pl.when(i == 0)
    def _():
        ext_ref[0:8, :] = buf_ref[0]
        hc_ref[...] = h0_ref[0]

    @pl.when(i > 0)
    def _():
        ext_ref[0:8, :] = ext_ref[tt:tt + 8, :]

    ext_ref[8:8 + tt, :] = xb_ref[0]
    cw = cw_ref[...]
    xc = cb_ref[...] + cw[0:1] * ext_ref[5:5 + tt, :]
    for j in range(1, CONV_W):
        xc = xc + cw[j:j + 1] * ext_ref[5 + j:5 + j + tt, :]
    xcb = xc.astype(BF16)
    rs, igs = [], []
    for n in range(LRU_BLOCKS):
        blk = xcb[:, n * LRU_BW:(n + 1) * LRU_BW]
        rs.append(_dot(blk, gw_ref[0, n]))
        igs.append(_dot(blk, gw_ref[1, n]))
    gb = gb_ref[...]
    r = _sigmoid(jnp.concatenate(rs, axis=1) + gb[0:1])
    ig = _sigmoid(jnp.concatenate(igs, axis=1) + gb[1:2])
    log_a = (-LRU_C * _softplus(-lam_ref[...])) * r
    a = jnp.exp(log_a)
    a_ref[...] = a
    u_ref[...] = jnp.sqrt(1.0 - a * a) * (ig * xc)

    def body(t, h):
        h = a_ref[pl.ds(t, 1), :] * h + u_ref[pl.ds(t, 1), :]
        u_ref[pl.ds(t, 1), :] = h
        return h

    hc_ref[...] = lax.fori_loop(0, tt, body, hc_ref[...], unroll=8)
    y_ref[0] = (u_ref[...] * _gelu_tanh(yb_ref[0])).astype(y_ref.dtype)

    @pl.when(i == pl.num_programs(1) - 1)
    def _():
        nbuf_ref[0] = ext_ref[8 + t_last - 3:8 + t_last, :]
        hl_ref[0] = u_ref[t_last - 1:t_last, :]


def lru_core(proj, conv_buf, h0, conv_w, conv_b, gate_w, gate_b, lam, t_valid, tt):
    b, t, _ = proj.shape
    w = D_MODEL
    nt = t // tt
    t_last = t_valid - (nt - 1) * tt
    buf8 = jnp.concatenate([jnp.zeros((b, 5, w), F32), conv_buf], axis=1)
    kern = functools.partial(_lru_kernel, tt=tt, t_last=t_last)
    return pl.pallas_call(
        kern,
        grid=(b, nt),
        in_specs=[pl.BlockSpec((1, tt, w), lambda bi, i: (bi, i, 0)),
                  pl.BlockSpec((1, tt, w), lambda bi, i: (bi, i, 1)),
                  pl.BlockSpec((1, 8, w), lambda bi, i: (bi, 0, 0)),
                  pl.BlockSpec((1, 1, w), lambda bi, i: (bi, 0, 0)),
                  pl.BlockSpec((CONV_W, w), lambda bi, i: (0, 0)),
                  pl.BlockSpec((1, w), lambda bi, i: (0, 0)),
                  pl.BlockSpec((2, LRU_BLOCKS, LRU_BW, LRU_BW), lambda bi, i: (0, 0, 0, 0)),
                  pl.BlockSpec((2, w), lambda bi, i: (0, 0)),
                  pl.BlockSpec((1, w), lambda bi, i: (0, 0))],
        out_specs=[pl.BlockSpec((1, tt, w), lambda bi, i: (bi, i, 0)),
                   pl.BlockSpec((1, 3, w), lambda bi, i: (bi, 0, 0)),
                   pl.BlockSpec((1, 1, w), lambda bi, i: (bi, 0, 0))],
        out_shape=[jax.ShapeDtypeStruct((b, t, w), BF16),
                   jax.ShapeDtypeStruct((b, 3, w), F32),
                   jax.ShapeDtypeStruct((b, 1, w), F32)],
        scratch_shapes=[pltpu.VMEM((tt + 8, w), F32), pltpu.VMEM((tt, w), F32),
                        pltpu.VMEM((tt, w), F32), pltpu.VMEM((1, w), F32)],
        compiler_params=_cparams(("parallel", "arbitrary")),
        name="lru_core",
    )(proj, proj, buf8, h0.reshape(b, 1, w), conv_w, conv_b.reshape(1, w), gate_w, gate_b,
      lam.reshape(1, w))


def lru_mixer(x, res_gain_pre, res_gain_post, conv_buf, h0, wts, tm, tt):
    w_in, conv_w, conv_b, gate_w, gate_b, lam, w_out = wts
    b, t, d = x.shape
    x2 = x.reshape(b * t, d)
    proj = norm_matmul(x2, res_gain_pre, w_in, tm, 512)
    tp = -(-t // 8) * 8
    proj3 = proj.reshape(b, t, 2 * d)
    if tp != t:
        proj3 = jnp.pad(proj3, ((0, 0), (0, tp - t), (0, 0)))
    y, nbuf, hl = lru_core(proj3, conv_buf, h0, conv_w, conv_b, gate_w, gate_b, lam, t, min(tt, tp))
    if tp != t:
        y = y[:, :t]
    out = matmul_norm_res(y.reshape(b * t, d), w_out, res_gain_post, x2, tm)
    return out.reshape(b, t, d), nbuf, hl.reshape(b, d)


GDN_W = GDN_HEADS * GDN_DH
GDN_PROJ = 4608
GDN_AB_BLOCK = 4 * GDN_W // 128
GDN_CHUNK = 128


def _gdn_prep_kernel(x_ref, ab_ref, buf_ref, cw_ref, ad_ref, q_ref, k_ref, v_ref, gb_ref, nbuf_ref,
                     ext_ref, *, tt, t_last):
    i = pl.program_id(1)

    @pl.when(i == 0)
    def _():
        ext_ref[0:8, :] = buf_ref[0]

    @pl.when(i > 0)
    def _():
        ext_ref[0:8, :] = ext_ref[tt:tt + 8, :]

    ext_ref[8:8 + tt, :] = x_ref[0]
    cw = cw_ref[...]
    xc = cw[0:1] * ext_ref[5:5 + tt, :]
    for j in range(1, CONV_W):
        xc = xc + cw[j:j + 1] * ext_ref[5 + j:5 + j + tt, :]
    xc = _silu(xc)
    for h in range(GDN_HEADS):
        qh = xc[:, h * GDN_DH:(h + 1) * GDN_DH]
        kh = xc[:, GDN_W + h * GDN_DH:GDN_W + (h + 1) * GDN_DH]
        qn = qh * lax.rsqrt(jnp.sum(qh * qh, axis=-1, keepdims=True) + EPS) * (GDN_DH ** -0.5)
        kn = kh * lax.rsqrt(jnp.sum(kh * kh, axis=-1, keepdims=True) + EPS)
        q_ref[0, :, h * GDN_DH:(h + 1) * GDN_DH] = qn
        k_ref[0, :, h * GDN_DH:(h + 1) * GDN_DH] = kn
    v_ref[0] = xc[:, 2 * GDN_W:]
    ab = ab_ref[0]
    ad = ad_ref[...]
    lane = lax.broadcasted_iota(jnp.int32, ab.shape, 1)
    g = -jnp.exp(ad[0:1]) * _softplus(ab + ad[1:2])
    gb_ref[0] = jnp.where(lane < GDN_HEADS, g, _sigmoid(ab))

    @pl.when(i == pl.num_programs(1) - 1)
    def _():
        nbuf_ref[0] = ext_ref[8 + t_last - 3:8 + t_last, :]


def gdn_prep(proj, conv_buf, conv_w, a_log, dt_bias, t_valid, tt):
    b, t, _ = proj.shape
    w3 = 3 * GDN_W
    nt = t // tt
    t_last = t_valid - (nt - 1) * tt
    buf8 = jnp.concatenate([jnp.zeros((b, 5, w3), F32), conv_buf], axis=1)
    ad = jnp.zeros((2, 128), F32).at[0, :GDN_HEADS].set(a_log).at[1, :GDN_HEADS].set(dt_bias)
    kern = functools.partial(_gdn_prep_kernel, tt=tt, t_last=t_last)
    act = jax.ShapeDtypeStruct((b, t, GDN_W), F32)
    return pl.pallas_call(
        kern,
        grid=(b, nt),
        in_specs=[pl.BlockSpec((1, tt, w3), lambda bi, i: (bi, i, 0)),
                  pl.BlockSpec((1, tt, 128), lambda bi, i: (bi, i, GDN_AB_BLOCK)),
                  pl.BlockSpec((1, 8, w3), lambda bi, i: (bi, 0, 0)),
                  pl.BlockSpec((CONV_W, w3), lambda bi, i: (0, 0)),
                  pl.BlockSpec((2, 128), lambda bi, i: (0, 0))],
        out_specs=[pl.BlockSpec((1, tt, GDN_W), lambda bi, i: (bi, i, 0)),
                   pl.BlockSpec((1, tt, GDN_W), lambda bi, i: (bi, i, 0)),
                   pl.BlockSpec((1, tt, GDN_W), lambda bi, i: (bi, i, 0)),
                   pl.BlockSpec((1, tt, 128), lambda bi, i: (bi, i, 0)),
                   pl.BlockSpec((1, 3, w3), lambda bi, i: (bi, 0, 0))],
        out_shape=[act, act, act, jax.ShapeDtypeStruct((b, t, 128), F32),
                   jax.ShapeDtypeStruct((b, 3, w3), F32)],
        scratch_shapes=[pltpu.VMEM((tt + 8, w3), F32)],
        compiler_params=_cparams(("parallel", "arbitrary")),
        name="gdn_prep",
    )(proj, proj, buf8, conv_w, ad)


def _gdn_out(o, gate, onw):
    on = o * lax.rsqrt(jnp.mean(o * o, axis=-1, keepdims=True) + EPS) * onw
    return on * _silu(gate)


def _bdot(a, b):
    return _dot(a.astype(BF16), b.astype(BF16))


def _unit_lower_inverses(ms, row, col):
    c = ms[0].shape[0]
    base = 16
    eye = (row == col).astype(F32)
    same = (row // base) == (col // base)
    ps = [-jnp.where(same, m, 0.0) for m in ms]
    ts = [eye + p for p in ps]
    s = 1
    while 2 * s < base:
        ps = [_bdot(p, p) for p in ps]
        ts = [t + _bdot(t, p) for t, p in zip(ts, ps)]
        s *= 2
    bs = base
    while bs < c:
        off = ((row // (2 * bs)) == (col // (2 * bs))) & ((row // bs) != (col // bs))
        tms = [_bdot(t, jnp.where(off, m, 0.0)) for t, m in zip(ts, ms)]
        ts = [t - _bdot(tm, t) for t, tm in zip(ts, tms)]
        bs *= 2
    return ts


def _gdn_chunk_kernel(q_ref, k_ref, v_ref, gb_ref, gate_ref, onw_ref, s0_ref, y_ref, sl_ref, s_ref):
    i = pl.program_id(1)
    c = GDN_CHUNK

    @pl.when(i == 0)
    def _():
        s_ref[...] = s0_ref[0]

    gb = gb_ref[0]
    row = lax.broadcasted_iota(jnp.int32, (c, c), 0)
    col = lax.broadcasted_iota(jnp.int32, (c, c), 1)
    gc = gb
    sft = 1
    while sft < c:
        gc = gc + jnp.where(row >= sft, pltpu.roll(gc, sft, axis=0), 0.0)
        sft *= 2
    gct = gc.T
    incl = row >= col
    strict = row > col
    onw = onw_ref[...]
    hs = range(GDN_HEADS)
    sls = [slice(h * GDN_DH, (h + 1) * GDN_DH) for h in hs]
    gcol = [gc[:, h:h + 1] for h in hs]
    decay = [jnp.exp(jnp.where(incl, gcol[h] - gct[h:h + 1, :], NEG)) for h in hs]
    beta = [gb[:, GDN_HEADS + h:GDN_HEADS + h + 1] for h in hs]
    egc = [jnp.exp(gcol[h]) for h in hs]
    glast = [gc[c - 1:c, h:h + 1] for h in hs]
    kh = [k_ref[0, :, sls[h]] for h in hs]
    khb = [kh[h].astype(BF16) for h in hs]
    kb = [kh[h] * beta[h] for h in hs]
    ms = [jnp.where(strict, _dot_nt(kb[h].astype(BF16), khb[h]) * decay[h], 0.0) for h in hs]
    attn = [(_dot_nt(q_ref[0, :, sls[h]].astype(BF16), khb[h]) * decay[h]).astype(BF16) for h in hs]
    tinv = [t.astype(BF16) for t in _unit_lower_inverses(ms, row, col)]
    sb = [s_ref[h].astype(BF16) for h in hs]
    w = [v_ref[0, :, sls[h]] * beta[h] - _dot((kb[h] * egc[h]).astype(BF16), sb[h]) for h in hs]
    ub = [_dot(tinv[h], w[h].astype(BF16)).astype(BF16) for h in hs]
    o = [_dot((q_ref[0, :, sls[h]] * egc[h]).astype(BF16), sb[h]) + _dot(attn[h], ub[h]) for h in hs]
    kdt = [(kh[h] * jnp.exp(glast[h] - gcol[h])).T.astype(BF16) for h in hs]
    for h in hs:
        s_ref[h] = s_ref[h] * jnp.exp(glast[h]) + _dot(kdt[h], ub[h])
    for h in hs:
        y_ref[0, :, sls[h]] = _gdn_out(o[h], gate_ref[0, :, sls[h]], onw).astype(y_ref.dtype)

    @pl.when(i == pl.num_programs(1) - 1)
    def _():
        sl_ref[0] = s_ref[...]


def gdn_chunked(q, k, v, gb, proj, o_norm_w, s0):
    b, t, _ = q.shape
    c = GDN_CHUNK
    blk = lambda bi, i: (bi, i, 0)
    return pl.pallas_call(
        _gdn_chunk_kernel,
        grid=(b, t // c),
        in_specs=[pl.BlockSpec((1, c, GDN_W), blk), pl.BlockSpec((1, c, GDN_W), blk),
                  pl.BlockSpec((1, c, GDN_W), blk), pl.BlockSpec((1, c, 128), blk),
                  pl.BlockSpec((1, c, GDN_W), lambda bi, i: (bi, i, 3)),
                  pl.BlockSpec((1, GDN_DH), lambda bi, i: (0, 0)),
                  pl.BlockSpec((1, GDN_HEADS, GDN_DH, GDN_DH), lambda bi, i: (bi, 0, 0, 0))],
        out_specs=[pl.BlockSpec((1, c, GDN_W), blk),
                   pl.BlockSpec((1, GDN_HEADS, GDN_DH, GDN_DH), lambda bi, i: (bi, 0, 0, 0))],
        out_shape=[jax.ShapeDtypeStruct((b, t, GDN_W), BF16),
                   jax.ShapeDtypeStruct((b, GDN_HEADS, GDN_DH, GDN_DH), F32)],
        scratch_shapes=[pltpu.VMEM((GDN_HEADS, GDN_DH, GDN_DH), F32)],
        compiler_params=_cparams(("parallel", "arbitrary")),
        name="gdn_chunked",
    )(q, k, v, gb, proj, o_norm_w.reshape(1, GDN_DH), s0)


def _gdn_steps_kernel(q_ref, k_ref, v_ref, gb_ref, gate_ref, onw_ref, s0_ref, y_ref, sl_ref, *, steps):
    gb = gb_ref[0]
    onw = onw_ref[...]
    zpad = jnp.zeros((GDN_DH - 8, GDN_DH), F32)
    for h in range(GDN_HEADS):
        sl = slice(h * GDN_DH, (h + 1) * GDN_DH)
        kt = jnp.concatenate([k_ref[0, :, sl], zpad], axis=0).T
        qt = jnp.concatenate([q_ref[0, :, sl], zpad], axis=0).T
        vh = v_ref[0, :, sl]
        s = s0_ref[0, h]
        outs = []
        for t in range(steps):
            kcol = kt[:, t:t + 1]
            qcol = qt[:, t:t + 1]
            s = s * jnp.exp(gb[t:t + 1, h:h + 1])
            u = gb[t:t + 1, GDN_HEADS + h:GDN_HEADS + h + 1] * (
                vh[t:t + 1, :] - jnp.sum(s * kcol, axis=0, keepdims=True))
            s = s + kcol * u
            outs.append(jnp.sum(s * qcol, axis=0, keepdims=True))
        outs.append(jnp.zeros((8 - steps, GDN_DH), F32))
        o = jnp.concatenate(outs, axis=0)
        y_ref[0, :, sl] = _gdn_out(o, gate_ref[0, :, sl], onw).astype(y_ref.dtype)
        sl_ref[0, h] = s


def gdn_steps(q, k, v, gb, proj, o_norm_w, s0, steps):
    b = q.shape[0]
    blk = lambda bi: (bi, 0, 0)
    kern = functools.partial(_gdn_steps_kernel, steps=steps)
    return pl.pallas_call(
        kern,
        grid=(b,),
        in_specs=[pl.BlockSpec((1, 8, GDN_W), blk), pl.BlockSpec((1, 8, GDN_W), blk),
                  pl.BlockSpec((1, 8, GDN_W), blk), pl.BlockSpec((1, 8, 128), blk),
                  pl.BlockSpec((1, 8, GDN_W), lambda bi: (bi, 0, 3)),
                  pl.BlockSpec((1, GDN_DH), lambda bi: (0, 0)),
                  pl.BlockSpec((1, GDN_HEADS, GDN_DH, GDN_DH), lambda bi: (bi, 0, 0, 0))],
        out_specs=[pl.BlockSpec((1, 8, GDN_W), blk),
                   pl.BlockSpec((1, GDN_HEADS, GDN_DH, GDN_DH), lambda bi: (bi, 0, 0, 0))],
        out_shape=[jax.ShapeDtypeStruct((b, 8, GDN_W), BF16),
                   jax.ShapeDtypeStruct((b, GDN_HEADS, GDN_DH, GDN_DH), F32)],
        compiler_params=_cparams(("parallel",)),
        name="gdn_steps",
    )(q, k, v, gb, proj, o_norm_w.reshape(1, GDN_DH), s0)


def gdn_mixer(x, gain_pre, gain_post, conv_buf, s0, wts, tm, tt):
    w_in, conv_w, a_log, dt_bias, o_norm_w, w_out = wts
    b, t, d = x.shape
    x2 = x.reshape(b * t, d)
    proj = norm_matmul(x2, gain_pre, w_in, tm, 512).reshape(b, t, GDN_PROJ)
    if t % GDN_CHUNK == 0:
        q, k, v, gb, nbuf = gdn_prep(proj, conv_buf, conv_w, a_log, dt_bias, t, tt)
        y, s_new = gdn_chunked(q, k, v, gb, proj, o_norm_w, s0)
    else:
        assert t <= 8
        proj = jnp.pad(proj, ((0, 0), (0, 8 - t), (0, 0)))
        q, k, v, gb, nbuf = gdn_prep(proj, conv_buf, conv_w, a_log, dt_bias, t, 8)
        y, s_new = gdn_steps(q, k, v, gb, proj, o_norm_w, s0, t)
        y = y[:, :t]
    out = matmul_norm_res(y.reshape(b * t, d), w_out, gain_post, x2, tm)
    return out.reshape(b, t, d), nbuf, s_new


NSA_PROJ = 3072
NSA_Q_COLS = NSA_HEADS * NSA_DH
NSA_GATE_COL = NSA_Q_COLS + 6 * NSA_KVW
NSA_QB = 128
NSA_TK = 1024


def _online(s, mask, v, m, l, acc):
    s = jnp.where(mask, s, NEG)
    m_new = jnp.maximum(m, jnp.max(s, axis=1, keepdims=True))
    alpha = jnp.exp2(m - m_new)
    p = jnp.where(mask, jnp.exp2(s - m_new), 0.0)
    l = alpha * l + jnp.sum(p, axis=1, keepdims=True)
    acc = alpha * acc + _dot(p.astype(BF16), v)
    return m_new, l, acc


def _select_top(score, lane_f, n_sel, n_lanes):
    sel = jnp.zeros(score.shape, F32)
    sc = score
    for _ in range(n_sel):
        mx = jnp.max(sc, axis=1, keepdims=True)
        idx = jnp.min(jnp.where(sc == mx, lane_f, float(n_lanes)), axis=1, keepdims=True)
        pick = lane_f == idx
        sel = jnp.where(pick, 1.0, sel)
        sc = jnp.where(pick, -3e38, sc)
    return sel


def _nsa_compress_kernel(x0_ref, x1_ref, x2_ref, x3_ref, wk_ref, wv_ref, pe_ref, ck_ref, cv_ref, acc_ref):
    a = pl.program_id(1)

    @pl.when(a == 0)
    def _():
        acc_ref[...] = jnp.zeros_like(acc_ref)

    for r in range(8):
        xk = jnp.concatenate([x0_ref[0, :, 0, r, :], x1_ref[0, :, 0, r, :]], axis=1)
        xv = jnp.concatenate([x2_ref[0, :, 0, r, :], x3_ref[0, :, 0, r, :]], axis=1)
        acc_ref[0] += _dot((xk + pe_ref[0, r]).astype(BF16), wk_ref[0, r])
        acc_ref[1] += _dot((xv + pe_ref[1, r]).astype(BF16), wv_ref[0, r])

    @pl.when(a == pl.num_programs(1) - 1)
    def _():
        ck_ref[0] = acc_ref[0]
        cv_ref[0] = acc_ref[1]


def nsa_compress(projb, wbd, pe4):
    b, nb = projb.shape[:2]
    kcol = NSA_Q_COLS // 128
    xspec = lambda c: pl.BlockSpec((1, nb, 1, 8, 128), lambda bi, a: (bi, 0, a, 0, kcol + c))
    return pl.pallas_call(
        _nsa_compress_kernel,
        grid=(b, NSA_BLOCK // 8),
        in_specs=[xspec(0), xspec(1), xspec(2), xspec(3),
                  pl.BlockSpec((1, 8, NSA_KVW, NSA_KVW), lambda bi, a: (0, a, 0, 0)),
                  pl.BlockSpec((1, 8, NSA_KVW, NSA_KVW), lambda bi, a: (1, a, 0, 0)),
                  pl.BlockSpec((2, 8, 1, NSA_KVW), lambda bi, a: (0, a, 0, 0))],
        out_specs=[pl.BlockSpec((1, nb, NSA_KVW), lambda bi, a: (bi, 0, 0)),
                   pl.BlockSpec((1, nb, NSA_KVW), lambda bi, a: (bi, 0, 0))],
        out_shape=[jax.ShapeDtypeStruct((b, nb, NSA_KVW), F32)] * 2,
        scratch_shapes=[pltpu.VMEM((2, nb, NSA_KVW), F32)],
        compiler_params=_cparams(("parallel", "arbitrary")),
        name="nsa_compress",
    )(projb, projb, projb, projb, wbd, wbd, pe4)


LOG2E = 1.4426950408889634
MASK_BIAS = -(2.0 ** 100)
NSA_WTILES = NSA_WINDOW // NSA_QB + 1


def _tile_heads(x):
    return jnp.concatenate([x] * NSA_HPG, axis=1)


def _query_columns(q):
    qt = (q * (NSA_SCALE * LOG2E)).T.astype(BF16)
    return jnp.concatenate([qt[h * NSA_DH:(h + 1) * NSA_DH] for h in range(NSA_HPG)], axis=1)


def _nsa_select_kernel(q_ref, ck_ref, cvt_ref, selb_ref, ocmp_ref, *, nb, n_sel):
    i = pl.program_id(1)
    qb, hpg = NSA_QB, NSA_HPG
    gs = range(NSA_GROUPS)
    tq = lax.broadcasted_iota(jnp.int32, (nb, qb), 1) + i * qb
    nblk = lax.broadcasted_iota(jnp.int32, (nb, qb), 0)
    nblk_f = nblk.astype(F32)
    ok_c = _tile_heads((nblk + 1) * NSA_BLOCK - 1 <= tq)
    cur = tq // NSA_BLOCK
    forced = (nblk == 0) | (nblk == cur) | (nblk == cur - 1)
    q_cols = [_query_columns(q_ref[0, :, g * NSA_KVW:(g + 1) * NSA_KVW]) for g in gs]
    s = [jnp.where(ok_c, _dot(ck_ref[0, g], q_cols[g]), NEG) for g in gs]
    p = [jnp.where(ok_c, jnp.exp2(s[g] - jnp.max(s[g], axis=0, keepdims=True)), 0.0) for g in gs]
    p = [p[g] / jnp.maximum(jnp.sum(p[g], axis=0, keepdims=True), 1e-30) for g in gs]
    for g in gs:
        ocmp_ref[0, g, 0] = _dot(cvt_ref[0, g], p[g].astype(BF16))
    imp = [p[g][:, 0:qb] for g in gs]
    for h in range(1, hpg):
        imp = [imp[g] + p[g][:, h * qb:(h + 1) * qb] for g in gs]
    sel = [forced.astype(F32) for g in gs]
    sc = [jnp.where(forced, -3e38, jnp.where(nblk > cur, -1.0, imp[g])) for g in gs]
    for _ in range(n_sel - 3):
        mx = [jnp.max(sc[g], axis=0, keepdims=True) for g in gs]
        idx = [jnp.min(jnp.where(sc[g] == mx[g], nblk_f, float(nb)), axis=0, keepdims=True) for g in gs]
        pick = [nblk_f == idx[g] for g in gs]
        sel = [jnp.where(pick[g], 1.0, sel[g]) for g in gs]
        sc = [jnp.where(pick[g], -3e38, sc[g]) for g in gs]
    for g in gs:
        selb_ref[0, g] = ((sel[g] - 1.0) * (-MASK_BIAS)).astype(BF16)


def nsa_select(proj3, ckg, cvt):
    b, t, _ = proj3.shape
    g, dh = NSA_GROUPS, NSA_DH
    nb = ckg.shape[2]
    n_sel = min(NSA_TOP_N, nb)
    assert n_sel >= 3
    kern = functools.partial(_nsa_select_kernel, nb=nb, n_sel=n_sel)
    return pl.pallas_call(
        kern,
        grid=(b, t // NSA_QB),
        in_specs=[pl.BlockSpec((1, NSA_QB, NSA_Q_COLS), lambda bi, i: (bi, i, 0)),
                  pl.BlockSpec((1, g, nb, dh), lambda bi, i: (bi, 0, 0, 0)),
                  pl.BlockSpec((1, g, dh, nb), lambda bi, i: (bi, 0, 0, 0))],
        out_specs=[pl.BlockSpec((1, g, nb, NSA_QB), lambda bi, i: (bi, 0, 0, i)),
                   pl.BlockSpec((1, g, 1, dh, NSA_HPG * NSA_QB), lambda bi, i: (bi, 0, i, 0, 0))],
        out_shape=[jax.ShapeDtypeStruct((b, g, nb, t), BF16),
                   jax.ShapeDtypeStruct((b, g, t // NSA_QB, dh, NSA_HPG * NSA_QB), F32)],
        compiler_params=_cparams(("parallel", "arbitrary")),
        name="nsa_select",
    )(proj3, ckg, cvt)


def _nsa_prompt_kernel(q_ref, gt_ref, selb_ref, ocmp_ref, oh_ref, kg_ref, vst_ref, vwt_ref, o_ref, acc_ref):
    i = pl.program_id(2)
    qb, dh, hpg, tk = NSA_QB, NSA_DH, NSA_HPG, NSA_TK
    q_cols = _query_columns(q_ref[0])
    rhs = jnp.concatenate([_tile_heads(selb_ref[0, 0]), q_cols], axis=0)
    tq_k = lax.broadcasted_iota(jnp.int32, (tk, qb), 1) + i * qb
    krow = lax.broadcasted_iota(jnp.int32, (tk, qb), 0)

    acc_ref[...] = jnp.zeros_like(acc_ref)

    half = hpg * qb // 2
    halves = (slice(0, half), slice(half, 2 * half))
    rhs_h = [rhs[:, cs] for cs in halves]

    def scores(kt):
        keys = jnp.concatenate([oh_ref[pl.ds(kt * tk, tk), :], kg_ref[0, 0, pl.ds(kt * tk, tk), 0:dh]], axis=1)
        return [_dot(keys, r) for r in rhs_h]

    def causal(sc, kpos0):
        bias = jnp.where(krow + kpos0 <= tq_k, 0.0, MASK_BIAS)
        bias = jnp.concatenate([bias] * (half // qb), axis=1)
        return [s + bias for s in sc]

    def absorb(sc, kt, ml):
        out = []
        for c, cs in enumerate(halves):
            m, l = ml[c]
            m_new = jnp.maximum(m, jnp.max(sc[c], axis=0, keepdims=True))
            alpha = jnp.exp2(m - m_new)
            pt = jnp.exp2(sc[c] - m_new)
            l = alpha * l + jnp.sum(pt, axis=0, keepdims=True)
            acc_ref[:, cs] = alpha * acc_ref[:, cs] + _dot(vst_ref[0, 0, kt], pt.astype(BF16))
            out.append((m_new, l))
        return tuple(out)

    def pair(j, ml):
        sa, sb = scores(2 * j), scores(2 * j + 1)
        return absorb(sb, 2 * j + 1, absorb(sa, 2 * j, ml))

    n_tiles = (i * qb + qb + tk - 1) // tk
    n_pairs = (n_tiles - 1) // 2
    ml0 = tuple((jnp.full((1, half), NEG, F32), jnp.zeros((1, half), F32)) for _ in halves)
    ml1 = lax.fori_loop(0, n_pairs, pair, ml0)
    kt_a = 2 * n_pairs
    kt_d = n_tiles - 1
    sa, sd = scores(kt_a), scores(kt_d)
    sa = causal(sa, jnp.where(kt_a < kt_d, kt_a * tk, 2 * kt_d * tk + 2 * tk))
    sd = causal(sd, kt_d * tk)
    ml2 = absorb(sd, kt_d, absorb(sa, kt_a, ml1))
    l_slc = jnp.concatenate([ml2[0][1], ml2[1][1]], axis=1)
    o_slc = acc_ref[...] / jnp.maximum(l_slc, 1e-30)
    o_cmp = ocmp_ref[0, 0, 0]

    w0 = jnp.maximum(i - (NSA_WTILES - 1), 0)
    wl = NSA_WTILES * qb
    sw = _dot(kg_ref[0, 0, pl.ds(w0 * qb, wl), dh:2 * dh], q_cols)
    dist = (lax.broadcasted_iota(jnp.int32, (wl, qb), 1) + i * qb
            - lax.broadcasted_iota(jnp.int32, (wl, qb), 0) - w0 * qb)
    sw = sw + _tile_heads(jnp.where((dist >= 0) & (dist < NSA_WINDOW), 0.0, MASK_BIAS))
    pw = jnp.exp2(sw - jnp.max(sw, axis=0, keepdims=True))
    lw = jnp.sum(pw, axis=0, keepdims=True)
    pw = pw.astype(BF16)
    o_win = _dot(vwt_ref[0, 0, w0], pw[0:qb])
    for jj in range(1, NSA_WTILES):
        o_win = o_win + _dot(vwt_ref[0, 0, w0 + jj], pw[jj * qb:(jj + 1) * qb])
    o_win = o_win / jnp.maximum(lw, 1e-30)

    gt = _sigmoid(gt_ref[0, 0])
    outs = []
    for h in range(hpg):
        cs = slice(h * qb, (h + 1) * qb)
        outs.append(gt[3 * h:3 * h + 1] * o_cmp[:, cs] + gt[3 * h + 1:3 * h + 2] * o_slc[:, cs]
                    + gt[3 * h + 2:3 * h + 3] * o_win[:, cs])
    o_ref[0] = jnp.concatenate(outs, axis=0).T.astype(o_ref.dtype)


def nsa_prompt_attn(proj3, gtt, selb, ocmp, onehot, kg, vst, vwt):
    b, t, _ = proj3.shape
    g = NSA_GROUPS
    nb = selb.shape[2]
    assert t % NSA_TK == 0 and t >= NSA_WTILES * NSA_QB
    per_bg = lambda bi, gi, i: (bi, gi, 0, 0)
    return pl.pallas_call(
        _nsa_prompt_kernel,
        grid=(b, g, t // NSA_QB),
        in_specs=[pl.BlockSpec((1, NSA_QB, NSA_KVW), lambda bi, gi, i: (bi, i, gi)),
                  pl.BlockSpec((1, 1, 3 * NSA_HPG, NSA_QB), lambda bi, gi, i: (bi, gi, 0, i)),
                  pl.BlockSpec((1, 1, nb, NSA_QB), lambda bi, gi, i: (bi, gi, 0, i)),
                  pl.BlockSpec((1, 1, 1, NSA_DH, NSA_HPG * NSA_QB), lambda bi, gi, i: (bi, gi, i, 0, 0)),
                  pl.BlockSpec((t, nb), lambda bi, gi, i: (0, 0)),
                  pl.BlockSpec((1, 1, t, 2 * NSA_DH), per_bg),
                  pl.BlockSpec((1, 1, t // NSA_TK, NSA_DH, NSA_TK), lambda bi, gi, i: (bi, gi, 0, 0, 0)),
                  pl.BlockSpec((1, 1, t // NSA_QB, NSA_DH, NSA_QB), lambda bi, gi, i: (bi, gi, 0, 0, 0))],
        out_specs=pl.BlockSpec((1, NSA_QB, NSA_KVW), lambda bi, gi, i: (bi, i, gi)),
        out_shape=jax.ShapeDtypeStruct((b, t, NSA_Q_COLS), BF16),
        scratch_shapes=[pltpu.VMEM((NSA_DH, NSA_HPG * NSA_QB), F32)],
        compiler_params=_cparams(("parallel", "parallel", "arbitrary")),
        name="nsa_prompt_attn",
    )(proj3, gtt, selb, ocmp, onehot, kg, vst, vwt)


def nsa_prompt(x, gain_pre, gain_post, wts, tm):
    w_in, wbd, pe4, w_out = wts
    b, t, d = x.shape
    x2 = x.reshape(b * t, d)
    proj = norm_matmul(x2, gain_pre, w_in, tm, 512)
    proj3 = proj.reshape(b, t, NSA_PROJ)
    nb = t // NSA_BLOCK
    ck, cv = nsa_compress(proj.reshape(b, nb, NSA_BLOCK // 8, 8, NSA_PROJ), wbd, pe4)
    g, dh = NSA_GROUPS, NSA_DH
    ckg = ck.reshape(b, nb, g, dh).transpose(0, 2, 1, 3).astype(BF16)
    cvt = cv.reshape(b, nb, g, dh).transpose(0, 2, 3, 1).astype(BF16)
    c0 = NSA_Q_COLS
    kv_rows = proj3[:, :, c0:c0 + 4 * NSA_KVW].reshape(b, t, 4, g, dh)
    wb = min(NSA_WINDOW, t)
    win_rows = proj3[:, t - wb:, c0 + 4 * NSA_KVW:c0 + 6 * NSA_KVW].reshape(b, wb, 2, g, dh)
    kv6 = proj3[:, :, c0 + 2 * NSA_KVW:c0 + 6 * NSA_KVW].reshape(b, t, 4, g, dh).astype(BF16)
    kg = jnp.stack([kv6[:, :, 0], kv6[:, :, 2]], axis=3).transpose(0, 2, 1, 3, 4).reshape(b, g, t, 2 * dh)
    vst = kv6[:, :, 1].reshape(b, t // NSA_TK, NSA_TK, g, dh).transpose(0, 3, 1, 4, 2)
    vwt = kv6[:, :, 3].reshape(b, t // NSA_QB, NSA_QB, g, dh).transpose(0, 3, 1, 4, 2)
    onehot = (jnp.arange(t)[:, None] // NSA_BLOCK == jnp.arange(nb)[None, :]).astype(BF16)
    gtt = proj3[:, :, NSA_GATE_COL:NSA_GATE_COL + 3 * NSA_HEADS].reshape(b, t, g, 3 * NSA_HPG).transpose(0, 2, 3, 1)
    selb, ocmp = nsa_select(proj3, ckg, cvt)
    o = nsa_prompt_attn(proj3, gtt, selb, ocmp, onehot, kg, vst, vwt)
    out = matmul_norm_res(o.reshape(b * t, NSA_Q_COLS), w_out, gain_post, x2, tm)
    return out.reshape(b, t, d), kv_rows, win_rows


def nsa_weights(w_in, cmp_pe, cmp_w, w_out):
    w_in_p = jnp.pad(w_in, ((0, 0), (0, NSA_PROJ - w_in.shape[1]))).astype(BF16)
    eye = jnp.eye(NSA_GROUPS, dtype=F32)
    wbd = jnp.einsum('gh,cpde->cpgdhe', eye, cmp_w).reshape(2, NSA_BLOCK, NSA_KVW, NSA_KVW).astype(BF16)
    pe4 = jnp.tile(cmp_pe.transpose(1, 0, 2)[:, :, None, :], (1, 1, 1, NSA_GROUPS))
    return w_in_p, wbd, pe4, w_out.astype(BF16)


PAGES_PER_STEP = 8
STG_PITCH = NSA_BLOCK + 4


def _nsa_compress_paged_kernel(pt_ref, *refs, nb, n_steps):
    pages = refs[:PAGES_PER_STEP]
    wk_ref, wv_ref, pe_ref, ck_ref, cv_ref, stg_ref, acc_ref = refs[PAGES_PER_STEP:]
    bi = pl.program_id(0)
    j = pl.program_id(1)
    page_rows = pages[0].shape[2]
    slot = bi % 2
    bpp = page_rows // NSA_BLOCK
    for r in range(PAGES_PER_STEP):
        for c in range(4):
            xt = pages[r][0, c * 128:(c + 1) * 128, :].T
            for k in range(bpp):
                row0 = pl.multiple_of((j * PAGES_PER_STEP + r) * (bpp * STG_PITCH), 8) + k * STG_PITCH
                stg_ref[slot * 4 + c, pl.ds(row0, NSA_BLOCK), :] = xt[k * NSA_BLOCK:(k + 1) * NSA_BLOCK]

    @pl.when(bi > 0)
    def _():
        @pl.when(j == 0)
        def _():
            acc_ref[...] = jnp.zeros_like(acc_ref)

        prev = 1 - slot
        pps = wk_ref.shape[1]
        for r in range(pps):
            p = j * pps + r
            rows_p = pl.ds(p, nb, stride=STG_PITCH)
            xk = jnp.concatenate([stg_ref[prev * 4, rows_p, :], stg_ref[prev * 4 + 1, rows_p, :]], axis=1)
            xv = jnp.concatenate([stg_ref[prev * 4 + 2, rows_p, :], stg_ref[prev * 4 + 3, rows_p, :]], axis=1)
            acc_ref[0] += _dot((xk + pe_ref[0, r]).astype(BF16), wk_ref[0, r])
            acc_ref[1] += _dot((xv + pe_ref[1, r]).astype(BF16), wv_ref[0, r])

        @pl.when(j == n_steps - 1)
        def _():
            ck_ref[0] = acc_ref[0]
            cv_ref[0] = acc_ref[1]


def nsa_compress_paged(cache2, page_table, wbd, pe4):
    b, n_pages = page_table.shape
    page_rows = cache2.shape[2]
    past = n_pages * page_rows
    nb = past // NSA_BLOCK
    n_steps = n_pages // PAGES_PER_STEP
    pps = NSA_BLOCK // n_steps
    assert n_pages % PAGES_PER_STEP == 0 and NSA_BLOCK % n_steps == 0 and page_rows % NSA_BLOCK == 0
    pt = page_table.reshape(-1)

    def page_map(r):
        return lambda bi, j, pt_ref: (pt_ref[jnp.minimum(bi, b - 1) * n_pages + j * PAGES_PER_STEP + r], 0, 0)

    out_map = lambda bi, j, pt_ref: (jnp.maximum(bi - 1, 0), 0, 0)
    kern = functools.partial(_nsa_compress_paged_kernel, nb=nb, n_steps=n_steps)
    grid_spec = pltpu.PrefetchScalarGridSpec(
        num_scalar_prefetch=1,
        grid=(b + 1, n_steps),
        in_specs=[pl.BlockSpec((1, 2 * NSA_KVW, page_rows), page_map(r)) for r in range(PAGES_PER_STEP)]
        + [pl.BlockSpec((1, pps, NSA_KVW, NSA_KVW), lambda bi, j, pt_ref: (0, j, 0, 0)),
           pl.BlockSpec((1, pps, NSA_KVW, NSA_KVW), lambda bi, j, pt_ref: (1, j, 0, 0)),
           pl.BlockSpec((2, pps, 1, NSA_KVW), lambda bi, j, pt_ref: (0, j, 0, 0))],
        out_specs=[pl.BlockSpec((1, nb, NSA_KVW), out_map), pl.BlockSpec((1, nb, NSA_KVW), out_map)],
        scratch_shapes=[pltpu.VMEM((8, nb * STG_PITCH, 128), F32), pltpu.VMEM((2, nb, NSA_KVW), F32)],
    )
    return pl.pallas_call(
        kern,
        grid_spec=grid_spec,
        out_shape=[jax.ShapeDtypeStruct((b, nb, NSA_KVW), F32)] * 2,
        compiler_params=_cparams(("arbitrary", "arbitrary")),
        name="nsa_compress_paged",
    )(pt, *([cache2] * PAGES_PER_STEP), wbd, wbd, pe4)


def _diag_blocks(x, row_g):
    out = jnp.zeros((x.shape[0], NSA_DH), F32)
    for g in range(NSA_GROUPS):
        out = out + jnp.where(row_g == g, x[:, g * NSA_DH:(g + 1) * NSA_DH], 0.0)
    return out


def _nsa_sample_kernel(pt_ref, *refs, nb, ts, past, n_steps):
    pages = refs[:PAGES_PER_STEP]
    (q_ref, gt_ref, ck_ref, cv_ref, new_ref, win_ref, oh_ref, o_ref,
     bias_ref, ocmp_ref, m_ref, l_ref, acc_ref) = refs[PAGES_PER_STEP:]
    j = pl.program_id(1)
    rows = NSA_HEADS * ts
    page_rows = pages[0].shape[2]
    qbd = (q_ref[0] * (NSA_SCALE * LOG2E)).astype(BF16)
    rid = lax.broadcasted_iota(jnp.int32, (rows, 1), 0)
    row_t = rid % ts
    row_g = (rid // ts) % NSA_GROUPS

    @pl.when(j == 0)
    def _():
        nblk = lax.broadcasted_iota(jnp.int32, (rows, nb), 1)
        qpos = past + row_t
        ok_c = (nblk + 1) * NSA_BLOCK - 1 <= qpos
        s = jnp.where(ok_c, _dot_nt(qbd, ck_ref[0].astype(BF16)), NEG)
        p = jnp.where(ok_c, jnp.exp2(s - jnp.max(s, axis=1, keepdims=True)), 0.0)
        p = p / jnp.maximum(jnp.sum(p, axis=1, keepdims=True), 1e-30)
        ocmp_ref[...] = _diag_blocks(_dot(p.astype(BF16), cv_ref[0].astype(BF16)), row_g)
        gt_rows = NSA_GROUPS * ts
        imp = p[0:gt_rows]
        for h in range(1, NSA_HPG):
            imp = imp + p[h * gt_rows:(h + 1) * gt_rows]
        nb1 = lax.broadcasted_iota(jnp.int32, (gt_rows, nb), 1)
        cur = (past + lax.broadcasted_iota(jnp.int32, (gt_rows, nb), 0) % ts) // NSA_BLOCK
        forced = (nb1 == 0) | (nb1 == cur) | (nb1 == cur - 1)
        score = jnp.where(nb1 > cur, -1.0, jnp.where(forced, NSA_FORCE, imp))
        sel = _select_top(score, nb1.astype(F32), NSA_TOP_N - 1, nb)
        selb = jnp.concatenate([((sel - 1.0) * (-MASK_BIAS)).astype(BF16)] * NSA_HPG, axis=0)
        bias = _dot(selb, oh_ref[...])
        for pg in range(bias_ref.shape[0]):
            bias_ref[pg] = bias[:, pg * page_rows:(pg + 1) * page_rows]
        m_ref[...] = jnp.full(m_ref.shape, NEG, F32)
        l_ref[...] = jnp.zeros_like(l_ref)
        acc_ref[...] = jnp.zeros_like(acc_ref)

    pg0 = j * PAGES_PER_STEP
    sc = jnp.concatenate(
        [_dot(qbd, pages[r][0, 0:NSA_KVW, :].astype(BF16)) + bias_ref[pg0 + r] for r in range(PAGES_PER_STEP)],
        axis=1)
    m_old = m_ref[...]
    m_new = jnp.maximum(m_old, jnp.max(sc, axis=1, keepdims=True))
    alpha = jnp.exp2(m_old - m_new)
    p = jnp.exp2(sc - m_new)
    l_ref[...] = alpha * l_ref[...] + jnp.sum(p, axis=1, keepdims=True)
    m_ref[...] = m_new
    pb = p.astype(BF16)
    pv = _dot_nt(pb[:, 0:page_rows], pages[0][0, NSA_KVW:2 * NSA_KVW, :].astype(BF16))
    for r in range(1, PAGES_PER_STEP):
        pv = pv + _dot_nt(pb[:, r * page_rows:(r + 1) * page_rows],
                          pages[r][0, NSA_KVW:2 * NSA_KVW, :].astype(BF16))
    acc_ref[...] = alpha * acc_ref[...] + pv

    @pl.when(j == n_steps - 1)
    def _():
        new = new_ref[0]
        kidx = lax.broadcasted_iota(jnp.int32, (rows, 8), 1)
        new_ok = (kidx <= row_t) & (kidx < ts)
        ms, ls, accs = _online(_dot_nt(qbd, new[:, 0:NSA_KVW].astype(BF16)), new_ok,
                               new[:, NSA_KVW:2 * NSA_KVW].astype(BF16), m_ref[...], l_ref[...], acc_ref[...])
        o_slc = _diag_blocks(accs, row_g) / jnp.maximum(ls, 1e-30)
        win = win_ref[0]
        wl = win.shape[1]
        wr = lax.broadcasted_iota(jnp.int32, (rows, wl), 1)
        dist = row_t + wl - wr
        mw = jnp.full((rows, 1), NEG, F32)
        lw = jnp.zeros((rows, 1), F32)
        aw = jnp.zeros((rows, NSA_KVW), F32)
        sw = jnp.where((dist >= 0) & (dist < NSA_WINDOW), _dot(qbd, win[0:NSA_KVW].astype(BF16)), NEG)
        mw = jnp.max(sw, axis=1, keepdims=True)
        pw = jnp.where((dist >= 0) & (dist < NSA_WINDOW), jnp.exp2(sw - mw), 0.0)
        lw = jnp.sum(pw, axis=1, keepdims=True)
        aw = _dot_nt(pw.astype(BF16), win[NSA_KVW:2 * NSA_KVW].astype(BF16))
        mw, lw, aw = _online(_dot_nt(qbd, new[:, 2 * NSA_KVW:3 * NSA_KVW].astype(BF16)), new_ok,
                             new[:, 3 * NSA_KVW:4 * NSA_KVW].astype(BF16), mw, lw, aw)
        o_win = _diag_blocks(aw, row_g) / jnp.maximum(lw, 1e-30)
        gt = _sigmoid(gt_ref[0])
        o_ref[0] = gt[:, 0:1] * ocmp_ref[...] + gt[:, 1:2] * o_slc + gt[:, 2:3] * o_win


def nsa_sample_attn(cache2, page_table, qbd, gt, ck, cv, new_kv, win2, ts):
    b, n_pages = page_table.shape
    page_rows = cache2.shape[2]
    past = n_pages * page_rows
    nb = past // NSA_BLOCK
    n_steps = n_pages // PAGES_PER_STEP
    rows = NSA_HEADS * ts
    assert n_pages % PAGES_PER_STEP == 0 and past % NSA_BLOCK == 0 and ts <= 8 and nb >= NSA_TOP_N
    pt = page_table.reshape(-1)
    onehot_t = (jnp.arange(nb)[:, None] == jnp.arange(past)[None, :] // NSA_BLOCK).astype(BF16)

    def page_map(r):
        return lambda bi, j, pt_ref: (pt_ref[bi * n_pages + j * PAGES_PER_STEP + r], 1, 0)

    per_b = lambda bi, j, pt_ref: (bi, 0, 0)
    kern = functools.partial(_nsa_sample_kernel, nb=nb, ts=ts, past=past, n_steps=n_steps)
    grid_spec = pltpu.PrefetchScalarGridSpec(
        num_scalar_prefetch=1,
        grid=(b, n_steps),
        in_specs=[pl.BlockSpec((1, 2 * NSA_KVW, page_rows), page_map(r)) for r in range(PAGES_PER_STEP)]
        + [pl.BlockSpec((1, rows, NSA_KVW), per_b), pl.BlockSpec((1, rows, 3), per_b),
           pl.BlockSpec((1, nb, NSA_KVW), per_b), pl.BlockSpec((1, nb, NSA_KVW), per_b),
           pl.BlockSpec((1, 8, 4 * NSA_KVW), per_b),
           pl.BlockSpec((1, 2 * NSA_KVW, win2.shape[2]), per_b),
           pl.BlockSpec((nb, past), lambda bi, j, pt_ref: (0, 0))],
        out_specs=pl.BlockSpec((1, rows, NSA_DH), per_b),
        scratch_shapes=[pltpu.VMEM((n_pages, rows, page_rows), F32), pltpu.VMEM((rows, NSA_DH), F32),
                        pltpu.VMEM((rows, 1), F32), pltpu.VMEM((rows, 1), F32),
                        pltpu.VMEM((rows, NSA_KVW), F32)],
    )
    return pl.pallas_call(
        kern,
        grid_spec=grid_spec,
        out_shape=jax.ShapeDtypeStruct((b, rows, NSA_DH), F32),
        compiler_params=_cparams(("parallel", "arbitrary")),
        name="nsa_sample_attn",
    )(pt, *([cache2] * PAGES_PER_STEP), qbd, gt, ck, cv, new_kv, win2, onehot_t)


def _win_shift_kernel(win_ref, new_ref, o_ref, *, ts):
    w, wl = win_ref.shape[1], win_ref.shape[2]
    rolled = pltpu.roll(win_ref[0], wl - ts, axis=1)
    new_t = jnp.concatenate([new_ref[0], jnp.zeros((128 - 8, w), F32)], axis=0).T
    new_t = pltpu.roll(new_t, 128 - ts, axis=1)
    lane = lax.broadcasted_iota(jnp.int32, (w, 128), 1)
    o_ref[0, :, 0:wl - 128] = rolled[:, 0:wl - 128]
    o_ref[0, :, wl - 128:wl] = jnp.where(lane >= 128 - ts, new_t, rolled[:, wl - 128:wl])


def win_shift(win2, new_win, ts):
    b, w, wl = win2.shape
    return pl.pallas_call(
        functools.partial(_win_shift_kernel, ts=ts),
        grid=(b,),
        in_specs=[pl.BlockSpec((1, w, wl), lambda bi: (bi, 0, 0)),
                  pl.BlockSpec((1, 8, w), lambda bi: (bi, 0, 0))],
        out_specs=pl.BlockSpec((1, w, wl), lambda bi: (bi, 0, 0)),
        out_shape=jax.ShapeDtypeStruct((b, w, wl), F32),
        compiler_params=_cparams(("parallel",)),
        name="win_shift",
    )(win2, new_win)


def nsa_sample(x, gain_pre, gain_post, cache_kv, cache_win, page_table, wts):
    w_in, wbd, pe4, w_out = wts
    b, ts, d = x.shape
    g, hpg, dh = NSA_GROUPS, NSA_HPG, NSA_DH
    x2 = x.reshape(b * ts, d)
    proj = norm_matmul(x2, gain_pre, w_in, b * ts, 512).reshape(b, ts, NSA_PROJ)
    c0 = NSA_Q_COLS
    kv_rows = proj[:, :, c0:c0 + 4 * NSA_KVW].reshape(b, ts, 4, g, dh)
    n_pool, page_rows = cache_kv.shape[:2]
    cache2 = cache_kv.transpose(0, 2, 3, 4, 1).reshape(n_pool, 4 * NSA_KVW, page_rows)
    wl = cache_win.shape[1]
    win2 = cache_win.transpose(0, 2, 3, 4, 1).reshape(b, 2 * NSA_KVW, wl)
    new_kv = jnp.pad(proj[:, :, c0 + 2 * NSA_KVW:c0 + 6 * NSA_KVW], ((0, 0), (0, 8 - ts), (0, 0)))
    win_rows = win_shift(win2, new_kv[:, :, 2 * NSA_KVW:], ts).reshape(b, 2, g, dh, wl).transpose(0, 4, 1, 2, 3)
    ck, cv = nsa_compress_paged(cache2, page_table, wbd, pe4)
    q = proj[:, :, :c0].reshape(b, ts, g, hpg, dh).transpose(0, 3, 2, 1, 4)
    qbd = jnp.einsum('bhgtd,gk->bhgtkd', q, jnp.eye(g, dtype=F32)).reshape(b, hpg * g * ts, g * dh)
    gt = proj[:, :, NSA_GATE_COL:NSA_GATE_COL + 3 * NSA_HEADS].reshape(b, ts, g, hpg, 3)
    gt = gt.transpose(0, 3, 2, 1, 4).reshape(b, hpg * g * ts, 3)
    o = nsa_sample_attn(cache2, page_table, qbd, gt, ck, cv, new_kv, win2, ts)
    o = o.reshape(b, hpg, g, ts, dh).transpose(0, 3, 2, 1, 4).reshape(b * ts, NSA_Q_COLS).astype(BF16)
    out = matmul_norm_res(o, w_out, gain_post, x2, b * ts)
    return out.reshape(b, ts, d), kv_rows, win_rows


def kernel(x_prompt, x_sample, state_l0_lru_conv, state_l0_lru_h, cache_l1_kv, cache_l1_win, state_l2_gdn_conv, state_l2_gdn_s, state_l3_lru_conv, state_l3_lru_h, page_table, norm_gains, ffn_w_gate_up, ffn_w_down, l0_w_in, l0_conv_w, l0_conv_b, l0_gate_w, l0_gate_b, l0_lam, l0_w_out, l1_w_in, l1_cmp_pe, l1_cmp_w, l1_w_out, l2_w_in, l2_conv_w, l2_a_log, l2_dt_bias, l2_o_norm_w, l2_w_out, l3_w_in, l3_conv_w, l3_conv_b, l3_gate_w, l3_gate_b, l3_lam, l3_w_out):
    bp, tp, d = x_prompt.shape
    bs, ts, _ = x_sample.shape
    tm_p = 1024
    tm_s = bs * ts
    lru_w = {
        0: (l0_w_in.astype(BF16), l0_conv_w, l0_conv_b, l0_gate_w.astype(BF16), l0_gate_b, l0_lam,
            l0_w_out.astype(BF16)),
        3: (l3_w_in.astype(BF16), l3_conv_w, l3_conv_b, l3_gate_w.astype(BF16), l3_gate_b, l3_lam,
            l3_w_out.astype(BF16)),
    }
    lru_s = {0: (state_l0_lru_conv, state_l0_lru_h), 3: (state_l3_lru_conv, state_l3_lru_h)}
    nsa_w = nsa_weights(l1_w_in, l1_cmp_pe, l1_cmp_w, l1_w_out)
    gdn_w = (jnp.pad(l2_w_in, ((0, 0), (0, GDN_PROJ - l2_w_in.shape[1]))).astype(BF16), l2_conv_w, l2_a_log,
             l2_dt_bias, l2_o_norm_w, l2_w_out.astype(BF16))
    xp, xs = x_prompt, x_sample
    new = {}
    for layer in range(4):
        g = norm_gains[layer]
        kind = layer % 3
        if kind == 0:
            conv_s, h_s = lru_s[layer]
            xp, cp, hp = lru_mixer(xp, g[0], g[1], jnp.zeros((bp, CONV_W - 1, d), F32), jnp.zeros((bp, d), F32),
                                   lru_w[layer], tm_p, 256)
            xs, cs, hs = lru_mixer(xs, g[0], g[1], conv_s, h_s, lru_w[layer], tm_s, 256)
            new[layer] = (cp, hp, cs, hs)
        elif kind == 1:
            xp, kv_p, win_p = nsa_prompt(xp, g[0], g[1], nsa_w, tm_p)
            xs, kv_s, win_s = nsa_sample(xs, g[0], g[1], cache_l1_kv, cache_l1_win, page_table, nsa_w)
            new[layer] = (kv_p, win_p, kv_s, win_s)
        else:
            xp, cp, sp = gdn_mixer(xp, g[0], g[1], jnp.zeros((bp, CONV_W - 1, 3 * GDN_W), F32),
                                   jnp.zeros((bp, GDN_HEADS, GDN_DH, GDN_DH), F32), gdn_w, tm_p, 256)
            xs, cs, ss = gdn_mixer(xs, g[0], g[1], state_l2_gdn_conv, state_l2_gdn_s, gdn_w, tm_s, 8)
            new[layer] = (cp, sp, cs, ss)
        wgu = ffn_w_gate_up[layer].astype(BF16)
        wd = ffn_w_down[layer].astype(BF16)
        xp = ffn(xp.reshape(bp * tp, d), g[2], g[3], wgu, wd, tm_p).reshape(bp, tp, d)
        xs = ffn(xs.reshape(bs * ts, d), g[2], g[3], wgu, wd, tm_s).reshape(bs, ts, d)
    return (xp, xs) + new[0] + new[1] + new[2] + new[3]
```

```python
import functools

import jax
import jax.numpy as jnp
from jax import lax
from jax.experimental import pallas as pl
from jax.experimental.pallas import tpu as pltpu

F32 = jnp.float32
BF16 = jnp.bfloat16

D_MODEL = 1024
D_FF = 2816
EPS = 1e-6
CONV_W = 4
LRU_BLOCKS = 4
LRU_BW = 256
LRU_C = 8.0
NSA_HEADS = 16
NSA_DH = 64
NSA_GROUPS = 4
NSA_HPG = 4
NSA_KVW = 256
NSA_BLOCK = 64
NSA_TOP_N = 16
NSA_WINDOW = 512
NSA_FORCE = 1e4
NSA_SCALE = NSA_DH ** -0.5
GDN_HEADS = 8
GDN_DH = 128
NEG = -1e30

VMEM_LIMIT = 52 * 1024 * 1024


def _cparams(sem):
    return pltpu.CompilerParams(dimension_semantics=sem, vmem_limit_bytes=VMEM_LIMIT)


def _dot(a, b):
    return jnp.dot(a, b, preferred_element_type=F32)


def _dot_nt(a, b):
    return lax.dot_general(a, b, (((1,), (1,)), ((), ())), preferred_element_type=F32)


def _sigmoid(x):
    return 1.0 / (1.0 + jnp.exp(-x))


def _silu(x):
    return x * _sigmoid(x)


def _softplus(x):
    return jnp.maximum(x, 0.0) + jnp.log(1.0 + jnp.exp(-jnp.abs(x)))


def _gelu_tanh(x):
    c = 0.7978845608028654
    return 0.5 * x * (1.0 + jnp.tanh(c * (x + 0.044715 * x * x * x)))


def _rms(x, g):
    return x * lax.rsqrt(jnp.mean(x * x, axis=-1, keepdims=True) + EPS) * g


PROJ_TM = 512


def _norm_matmul_kernel(x_ref, g_ref, w_ref, o_ref):
    o_ref[...] = _dot(_rms(x_ref[...], g_ref[...]).astype(BF16), w_ref[...]).astype(o_ref.dtype)


def norm_matmul(x, gain, w, tm, out_dtype=F32):
    m, d = x.shape
    n = w.shape[1]
    tm = min(tm, m)
    return pl.pallas_call(
        _norm_matmul_kernel,
        grid=(m // tm,),
        in_specs=[pl.BlockSpec((tm, d), lambda i: (i, 0)),
                  pl.BlockSpec((1, d), lambda i: (0, 0)),
                  pl.BlockSpec((d, n), lambda i: (0, 0))],
        out_specs=pl.BlockSpec((tm, n), lambda i: (i, 0)),
        out_shape=jax.ShapeDtypeStruct((m, n), out_dtype),
        compiler_params=_cparams(("parallel",)),
        name="norm_matmul",
    )(x, gain.reshape(1, d), w)


def _matmul_norm_res_kernel(y_ref, w_ref, g_ref, r_ref, o_ref):
    m = _dot(y_ref[...], w_ref[...].astype(BF16))
    o_ref[...] = r_ref[...] + _rms(m, g_ref[...])


def matmul_norm_res(y, w, gain, res, tm):
    m, k = y.shape
    d = w.shape[1]
    return pl.pallas_call(
        _matmul_norm_res_kernel,
        grid=(m // tm,),
        in_specs=[pl.BlockSpec((tm, k), lambda i: (i, 0)),
                  pl.BlockSpec((k, d), lambda i: (0, 0)),
                  pl.BlockSpec((1, d), lambda i: (0, 0)),
                  pl.BlockSpec((tm, d), lambda i: (i, 0))],
        out_specs=pl.BlockSpec((tm, d), lambda i: (i, 0)),
        out_shape=jax.ShapeDtypeStruct((m, d), F32),
        compiler_params=_cparams(("parallel",)),
        name="matmul_norm_res",
    )(y, w, gain.reshape(1, d), res)


FFN_TF = 256


def _ffn_kernel(x_ref, gpre_ref, gpost_ref, wg_ref, wu_ref, wd_ref, o_ref, xn_ref, acc_ref):
    j = pl.program_id(1)

    @pl.when(j == 0)
    def _():
        xn_ref[...] = _rms(x_ref[...], gpre_ref[...]).astype(BF16)
        acc_ref[...] = jnp.zeros_like(acc_ref)

    xn = xn_ref[...]
    act = _silu(_dot(xn, wg_ref[...].astype(BF16))) * _dot(xn, wu_ref[...].astype(BF16))
    acc_ref[...] += _dot(act.astype(BF16), wd_ref[...].astype(BF16))

    @pl.when(j == pl.num_programs(1) - 1)
    def _():
        o_ref[...] = x_ref[...] + _rms(acc_ref[...], gpost_ref[...])


def ffn(x, g_pre, g_post, w_gate_up, w_down, tm):
    m, d = x.shape
    nf = D_FF // FFN_TF
    return pl.pallas_call(
        _ffn_kernel,
        grid=(m // tm, nf),
        in_specs=[pl.BlockSpec((tm, d), lambda i, j: (i, 0)),
                  pl.BlockSpec((1, d), lambda i, j: (0, 0)),
                  pl.BlockSpec((1, d), lambda i, j: (0, 0)),
                  pl.BlockSpec((d, FFN_TF), lambda i, j: (0, j)),
                  pl.BlockSpec((d, FFN_TF), lambda i, j: (0, j + nf)),
                  pl.BlockSpec((FFN_TF, d), lambda i, j: (j, 0))],
        out_specs=pl.BlockSpec((tm, d), lambda i, j: (i, 0)),
        out_shape=jax.ShapeDtypeStruct((m, d), F32),
        scratch_shapes=[pltpu.VMEM((tm, d), BF16), pltpu.VMEM((tm, d), F32)],
        compiler_params=_cparams(("parallel", "arbitrary")),
        name="ffn",
    )(x, g_pre.reshape(1, d), g_post.reshape(1, d), w_gate_up, w_gate_up, w_down)


def _lru_kernel(xb_ref, yb_ref, buf_ref, h0_ref, cw_ref, cb_ref, gw_ref, gb_ref, lam_ref,
                y_ref, nbuf_ref, hl_ref, ext_ref, a_ref, u_ref, hc_ref, *, tt, t_last):
    i = pl.program_id(1)
    w = D_MODEL

    @pl.when(i == 0)
    def _():
        ext_ref[0:8, :] = buf_ref[0]
        hc_ref[...] = h0_ref[0]

    @pl.when(i > 0)
    def _():
        ext_ref[0:8, :] = ext_ref[tt:tt + 8, :]

    ext_ref[8:8 + tt, :] = xb_ref[0]
    cw = cw_ref[...]
    xc = cb_ref[...] + cw[0:1] * ext_ref[5:5 + tt, :]
    for j in range(1, CONV_W):
        xc = xc + cw[j:j + 1] * ext_ref[5 + j:5 + j + tt, :]
    xcb = xc.astype(BF16)
    rs, igs = [], []
    for n in range(LRU_BLOCKS):
        blk = xcb[:, n * LRU_BW:(n + 1) * LRU_BW]
        rs.append(_dot(blk, gw_ref[0, n]))
        igs.append(_dot(blk, gw_ref[1, n]))
    gb = gb_ref[...]
    r = _sigmoid(jnp.concatenate(rs, axis=1) + gb[0:1])
    ig = _sigmoid(jnp.concatenate(igs, axis=1) + gb[1:2])
    log_a = (-LRU_C * _softplus(-lam_ref[...])) * r
    a = jnp.exp(log_a)
    a_ref[...] = a
    u_ref[...] = jnp.sqrt(1.0 - a * a) * (ig * xc)

    row8 = lax.broadcasted_iota(jnp.int32, (8, w), 0)

    def body(gi, h):
        r0 = pl.multiple_of(gi * 8, 8)
        a8 = a_ref[pl.ds(r0, 8), :]
        u8 = u_ref[pl.ds(r0, 8), :]
        for s in (1, 2, 4):
            a_sh = jnp.where(row8 >= s, pltpu.roll(a8, s, axis=0), 1.0)
            u_sh = jnp.where(row8 >= s, pltpu.roll(u8, s, axis=0), 0.0)
            u8 = a8 * u_sh + u8
            a8 = a8 * a_sh
        h8 = a8 * h + u8
        u_ref[pl.ds(r0, 8), :] = h8
        return h8[7:8, :]

    hc_ref[...] = lax.fori_loop(0, tt // 8, body, hc_ref[...], unroll=2)
    y_ref[0] = (u_ref[...] * _gelu_tanh(yb_ref[0])).astype(y_ref.dtype)

    @pl.when(i == pl.num_programs(1) - 1)
    def _():
        nbuf_ref[0] = ext_ref[8 + t_last - 3:8 + t_last, :]
        hl_ref[0] = u_ref[t_last - 1:t_last, :]


def lru_core(proj, conv_buf, h0, conv_w, conv_b, gate_w, gate_b, lam, t_valid, tt):
    b, t, _ = proj.shape
    w = D_MODEL
    nt = t // tt
    t_last = t_valid - (nt - 1) * tt
    buf8 = jnp.concatenate([jnp.zeros((b, 5, w), F32), conv_buf], axis=1)
    kern = functools.partial(_lru_kernel, tt=tt, t_last=t_last)
    return pl.pallas_call(
        kern,
        grid=(b, nt),
        in_specs=[pl.BlockSpec((1, tt, w), lambda bi, i: (bi, i, 0)),
                  pl.BlockSpec((1, tt, w), lambda bi, i: (bi, i, 1)),
                  pl.BlockSpec((1, 8, w), lambda bi, i: (bi, 0, 0)),
                  pl.BlockSpec((1, 1, w), lambda bi, i: (bi, 0, 0)),
                  pl.BlockSpec((CONV_W, w), lambda bi, i: (0, 0)),
                  pl.BlockSpec((1, w), lambda bi, i: (0, 0)),
                  pl.BlockSpec((2, LRU_BLOCKS, LRU_BW, LRU_BW), lambda bi, i: (0, 0, 0, 0)),
                  pl.BlockSpec((2, w), lambda bi, i: (0, 0)),
                  pl.BlockSpec((1, w), lambda bi, i: (0, 0))],
        out_specs=[pl.BlockSpec((1, tt, w), lambda bi, i: (bi, i, 0)),
                   pl.BlockSpec((1, 3, w), lambda bi, i: (bi, 0, 0)),
                   pl.BlockSpec((1, 1, w), lambda bi, i: (bi, 0, 0))],
        out_shape=[jax.ShapeDtypeStruct((b, t, w), BF16),
                   jax.ShapeDtypeStruct((b, 3, w), F32),
                   jax.ShapeDtypeStruct((b, 1, w), F32)],
        scratch_shapes=[pltpu.VMEM((tt + 8, w), F32), pltpu.VMEM((tt, w), F32),
                        pltpu.VMEM((tt, w), F32), pltpu.VMEM((1, w), F32)],
        compiler_params=_cparams(("parallel", "arbitrary")),
        name="lru_core",
    )(proj, proj, buf8, h0.reshape(b, 1, w), conv_w, conv_b.reshape(1, w), gate_w, gate_b,
      lam.reshape(1, w))


def lru_mixer(x, res_gain_pre, res_gain_post, conv_buf, h0, wts, tm, tt):
    w_in, conv_w, conv_b, gate_w, gate_b, lam, w_out = wts
    b, t, d = x.shape
    x2 = x.reshape(b * t, d)
    proj = norm_matmul(x2, res_gain_pre, w_in, PROJ_TM)
    tp = -(-t // 8) * 8
    proj3 = proj.reshape(b, t, 2 * d)
    if tp != t:
        proj3 = jnp.pad(proj3, ((0, 0), (0, tp - t), (0, 0)))
    y, nbuf, hl = lru_core(proj3, conv_buf, h0, conv_w, conv_b, gate_w, gate_b, lam, t, min(tt, tp))
    if tp != t:
        y = y[:, :t]
    out = matmul_norm_res(y.reshape(b * t, d), w_out, res_gain_post, x2, tm)
    return out.reshape(b, t, d), nbuf, hl.reshape(b, d)


GDN_W = GDN_HEADS * GDN_DH
GDN_PROJ = 4608
GDN_AB_BLOCK = 4 * GDN_W // 128
GDN_CHUNK = 128


def _gdn_prep_kernel(x_ref, ab_ref, buf_ref, cw_ref, ad_ref, q_ref, k_ref, v_ref, gb_ref, nbuf_ref,
                     ext_ref, *, tt, t_last):
    i = pl.program_id(1)

    @pl.when(i == 0)
    def _():
        ext_ref[0:8, :] = buf_ref[0]

    @pl.when(i > 0)
    def _():
        ext_ref[0:8, :] = ext_ref[tt:tt + 8, :]

    ext_ref[8:8 + tt, :] = x_ref[0]
    cw = cw_ref[...]
    xc = cw[0:1] * ext_ref[5:5 + tt, :]
    for j in range(1, CONV_W):
        xc = xc + cw[j:j + 1] * ext_ref[5 + j:5 + j + tt, :]
    xc = _silu(xc)
    for h in range(GDN_HEADS):
        qh = xc[:, h * GDN_DH:(h + 1) * GDN_DH]
        kh = xc[:, GDN_W + h * GDN_DH:GDN_W + (h + 1) * GDN_DH]
        qn = qh * lax.rsqrt(jnp.sum(qh * qh, axis=-1, keepdims=True) + EPS) * (GDN_DH ** -0.5)
        kn = kh * lax.rsqrt(jnp.sum(kh * kh, axis=-1, keepdims=True) + EPS)
        q_ref[0, :, h * GDN_DH:(h + 1) * GDN_DH] = qn
        k_ref[0, :, h * GDN_DH:(h + 1) * GDN_DH] = kn
    v_ref[0] = xc[:, 2 * GDN_W:]
    ab = ab_ref[0]
    ad = ad_ref[...]
    lane = lax.broadcasted_iota(jnp.int32, ab.shape, 1)
    g = -jnp.exp(ad[0:1]) * _softplus(ab + ad[1:2])
    gb_ref[0] = jnp.where(lane < GDN_HEADS, g, _sigmoid(ab))

    @pl.when(i == pl.num_programs(1) - 1)
    def _():
        nbuf_ref[0] = ext_ref[8 + t_last - 3:8 + t_last, :]


def gdn_prep(proj, conv_buf, conv_w, a_log, dt_bias, t_valid, tt):
    b, t, _ = proj.shape
    w3 = 3 * GDN_W
    nt = t // tt
    t_last = t_valid - (nt - 1) * tt
    buf8 = jnp.concatenate([jnp.zeros((b, 5, w3), F32), conv_buf], axis=1)
    ad = jnp.zeros((2, 128), F32).at[0, :GDN_HEADS].set(a_log).at[1, :GDN_HEADS].set(dt_bias)
    kern = functools.partial(_gdn_prep_kernel, tt=tt, t_last=t_last)
    act = jax.ShapeDtypeStruct((b, t, GDN_W), F32)
    return pl.pallas_call(
        kern,
        grid=(b, nt),
        in_specs=[pl.BlockSpec((1, tt, w3), lambda bi, i: (bi, i, 0)),
                  pl.BlockSpec((1, tt, 128), lambda bi, i: (bi, i, GDN_AB_BLOCK)),
                  pl.BlockSpec((1, 8, w3), lambda bi, i: (bi, 0, 0)),
                  pl.BlockSpec((CONV_W, w3), lambda bi, i: (0, 0)),
                  pl.BlockSpec((2, 128), lambda bi, i: (0, 0))],
        out_specs=[pl.BlockSpec((1, tt, GDN_W), lambda bi, i: (bi, i, 0)),
                   pl.BlockSpec((1, tt, GDN_W), lambda bi, i: (bi, i, 0)),
                   pl.BlockSpec((1, tt, GDN_W), lambda bi, i: (bi, i, 0)),
                   pl.BlockSpec((1, tt, 128), lambda bi, i: (bi, i, 0)),
                   pl.BlockSpec((1, 3, w3), lambda bi, i: (bi, 0, 0))],
        out_shape=[act, act, act, jax.ShapeDtypeStruct((b, t, 128), F32),
                   jax.ShapeDtypeStruct((b, 3, w3), F32)],
        scratch_shapes=[pltpu.VMEM((tt + 8, w3), F32)],
        compiler_params=_cparams(("parallel", "arbitrary")),
        name="gdn_prep",
    )(proj, proj, buf8, conv_w, ad)


def _gdn_out(o, gate, onw):
    on = o * lax.rsqrt(jnp.mean(o * o, axis=-1, keepdims=True) + EPS) * onw
    return on * _silu(gate)


def _bdot(a, b):
    return _dot(a.astype(BF16), b.astype(BF16))


def _unit_lower_inverses(ms, row, col):
    c = ms[0].shape[0]
    base = 16
    eye = (row == col).astype(F32)
    same = (row // base) == (col // base)
    ps = [-jnp.where(same, m, 0.0) for m in ms]
    ts = [eye + p for p in ps]
    s = 1
    while 2 * s < base:
        ps = [_bdot(p, p) for p in ps]
        ts = [t + _bdot(t, p) for t, p in zip(ts, ps)]
        s *= 2
    bs = base
    while bs < c:
        off = ((row // (2 * bs)) == (col // (2 * bs))) & ((row // bs) != (col // bs))
        tms = [_bdot(t, jnp.where(off, m, 0.0)) for t, m in zip(ts, ms)]
        ts = [t - _bdot(tm, t) for t, tm in zip(ts, tms)]
        bs *= 2
    return ts


def _gdn_chunk_kernel(q_ref, k_ref, v_ref, gb_ref, gate_ref, onw_ref, s0_ref, y_ref, sl_ref, s_ref):
    i = pl.program_id(1)
    c = GDN_CHUNK

    @pl.when(i == 0)
    def _():
        s_ref[...] = s0_ref[0]

    gb = gb_ref[0]
    row = lax.broadcasted_iota(jnp.int32, (c, c), 0)
    col = lax.broadcasted_iota(jnp.int32, (c, c), 1)
    gc = gb
    sft = 1
    while sft < c:
        gc = gc + jnp.where(row >= sft, pltpu.roll(gc, sft, axis=0), 0.0)
        sft *= 2
    gct = gc.T
    incl = row >= col
    strict = row > col
    onw = onw_ref[...]
    hs = range(GDN_HEADS)
    sls = [slice(h * GDN_DH, (h + 1) * GDN_DH) for h in hs]
    gcol = [gc[:, h:h + 1] for h in hs]
    decay = [jnp.exp(jnp.where(incl, gcol[h] - gct[h:h + 1, :], NEG)) for h in hs]
    beta = [gb[:, GDN_HEADS + h:GDN_HEADS + h + 1] for h in hs]
    egc = [jnp.exp(gcol[h]) for h in hs]
    glast = [gc[c - 1:c, h:h + 1] for h in hs]
    kh = [k_ref[0, :, sls[h]] for h in hs]
    khb = [kh[h].astype(BF16) for h in hs]
    kb = [kh[h] * beta[h] for h in hs]
    ms = [jnp.where(strict, _dot_nt(kb[h].astype(BF16), khb[h]) * decay[h], 0.0) for h in hs]
    attn = [(_dot_nt(q_ref[0, :, sls[h]].astype(BF16), khb[h]) * decay[h]).astype(BF16) for h in hs]
    tinv = [t.astype(BF16) for t in _unit_lower_inverses(ms, row, col)]
    sb = [s_ref[h].astype(BF16) for h in hs]
    w = [v_ref[0, :, sls[h]] * beta[h] - _dot((kb[h] * egc[h]).astype(BF16), sb[h]) for h in hs]
    ub = [_dot(tinv[h], w[h].astype(BF16)).astype(BF16) for h in hs]
    o = [_dot((q_ref[0, :, sls[h]] * egc[h]).astype(BF16), sb[h]) + _dot(attn[h], ub[h]) for h in hs]
    kdt = [(kh[h] * jnp.exp(glast[h] - gcol[h])).T.astype(BF16) for h in hs]
    for h in hs:
        s_ref[h] = s_ref[h] * jnp.exp(glast[h]) + _dot(kdt[h], ub[h])
    for h in hs:
        y_ref[0, :, sls[h]] = _gdn_out(o[h], gate_ref[0, :, sls[h]], onw).astype(y_ref.dtype)

    @pl.when(i == pl.num_programs(1) - 1)
    def _():
        sl_ref[0] = s_ref[...]


def gdn_chunked(q, k, v, gb, proj, o_norm_w, s0):
    b, t, _ = q.shape
    c = GDN_CHUNK
    blk = lambda bi, i: (bi, i, 0)
    return pl.pallas_call(
        _gdn_chunk_kernel,
        grid=(b, t // c),
        in_specs=[pl.BlockSpec((1, c, GDN_W), blk), pl.BlockSpec((1, c, GDN_W), blk),
                  pl.BlockSpec((1, c, GDN_W), blk), pl.BlockSpec((1, c, 128), blk),
                  pl.BlockSpec((1, c, GDN_W), lambda bi, i: (bi, i, 3)),
                  pl.BlockSpec((1, GDN_DH), lambda bi, i: (0, 0)),
                  pl.BlockSpec((1, GDN_HEADS, GDN_DH, GDN_DH), lambda bi, i: (bi, 0, 0, 0))],
        out_specs=[pl.BlockSpec((1, c, GDN_W), blk),
                   pl.BlockSpec((1, GDN_HEADS, GDN_DH, GDN_DH), lambda bi, i: (bi, 0, 0, 0))],
        out_shape=[jax.ShapeDtypeStruct((b, t, GDN_W), BF16),
                   jax.ShapeDtypeStruct((b, GDN_HEADS, GDN_DH, GDN_DH), F32)],
        scratch_shapes=[pltpu.VMEM((GDN_HEADS, GDN_DH, GDN_DH), F32)],
        compiler_params=_cparams(("parallel", "arbitrary")),
        name="gdn_chunked",
    )(q, k, v, gb, proj, o_norm_w.reshape(1, GDN_DH), s0)


def _gdn_steps_kernel(q_ref, k_ref, v_ref, gb_ref, gate_ref, onw_ref, s0_ref, y_ref, sl_ref, *, steps):
    gb = gb_ref[0]
    onw = onw_ref[...]
    zpad = jnp.zeros((GDN_DH - 8, GDN_DH), F32)
    for h in range(GDN_HEADS):
        sl = slice(h * GDN_DH, (h + 1) * GDN_DH)
        kt = jnp.concatenate([k_ref[0, :, sl], zpad], axis=0).T
        qt = jnp.concatenate([q_ref[0, :, sl], zpad], axis=0).T
        vh = v_ref[0, :, sl]
        s = s0_ref[0, h]
        outs = []
        for t in range(steps):
            kcol = kt[:, t:t + 1]
            qcol = qt[:, t:t + 1]
            s = s * jnp.exp(gb[t:t + 1, h:h + 1])
            u = gb[t:t + 1, GDN_HEADS + h:GDN_HEADS + h + 1] * (
                vh[t:t + 1, :] - jnp.sum(s * kcol, axis=0, keepdims=True))
            s = s + kcol * u
            outs.append(jnp.sum(s * qcol, axis=0, keepdims=True))
        outs.append(jnp.zeros((8 - steps, GDN_DH), F32))
        o = jnp.concatenate(outs, axis=0)
        y_ref[0, :, sl] = _gdn_out(o, gate_ref[0, :, sl], onw).astype(y_ref.dtype)
        sl_ref[0, h] = s


def gdn_steps(q, k, v, gb, proj, o_norm_w, s0, steps):
    b = q.shape[0]
    blk = lambda bi: (bi, 0, 0)
    kern = functools.partial(_gdn_steps_kernel, steps=steps)
    return pl.pallas_call(
        kern,
        grid=(b,),
        in_specs=[pl.BlockSpec((1, 8, GDN_W), blk), pl.BlockSpec((1, 8, GDN_W), blk),
                  pl.BlockSpec((1, 8, GDN_W), blk), pl.BlockSpec((1, 8, 128), blk),
                  pl.BlockSpec((1, 8, GDN_W), lambda bi: (bi, 0, 3)),
                  pl.BlockSpec((1, GDN_DH), lambda bi: (0, 0)),
                  pl.BlockSpec((1, GDN_HEADS, GDN_DH, GDN_DH), lambda bi: (bi, 0, 0, 0))],
        out_specs=[pl.BlockSpec((1, 8, GDN_W), blk),
                   pl.BlockSpec((1, GDN_HEADS, GDN_DH, GDN_DH), lambda bi: (bi, 0, 0, 0))],
        out_shape=[jax.ShapeDtypeStruct((b, 8, GDN_W), BF16),
                   jax.ShapeDtypeStruct((b, GDN_HEADS, GDN_DH, GDN_DH), F32)],
        compiler_params=_cparams(("parallel",)),
        name="gdn_steps",
    )(q, k, v, gb, proj, o_norm_w.reshape(1, GDN_DH), s0)


def gdn_mixer(x, gain_pre, gain_post, conv_buf, s0, wts, tm, tt):
    w_in, conv_w, a_log, dt_bias, o_norm_w, w_out = wts
    b, t, d = x.shape
    x2 = x.reshape(b * t, d)
    proj = norm_matmul(x2, gain_pre, w_in, PROJ_TM).reshape(b, t, GDN_PROJ)
    if t % GDN_CHUNK == 0:
        q, k, v, gb, nbuf = gdn_prep(proj, conv_buf, conv_w, a_log, dt_bias, t, tt)
        y, s_new = gdn_chunked(q, k, v, gb, proj, o_norm_w, s0)
    else:
        assert t <= 8
        proj = jnp.pad(proj, ((0, 0), (0, 8 - t), (0, 0)))
        q, k, v, gb, nbuf = gdn_prep(proj, conv_buf, conv_w, a_log, dt_bias, t, 8)
        y, s_new = gdn_steps(q, k, v, gb, proj, o_norm_w, s0, t)
        y = y[:, :t]
    out = matmul_norm_res(y.reshape(b * t, d), w_out, gain_post, x2, tm)
    return out.reshape(b, t, d), nbuf, s_new


NSA_PROJ = 3072
NSA_Q_COLS = NSA_HEADS * NSA_DH
NSA_GATE_COL = NSA_Q_COLS + 6 * NSA_KVW
NSA_QB = 128
NSA_TK = 1024


def _online(s, mask, v, m, l, acc):
    s = jnp.where(mask, s, NEG)
    m_new = jnp.maximum(m, jnp.max(s, axis=1, keepdims=True))
    alpha = jnp.exp2(m - m_new)
    p = jnp.where(mask, jnp.exp2(s - m_new), 0.0)
    l = alpha * l + jnp.sum(p, axis=1, keepdims=True)
    acc = alpha * acc + _dot(p.astype(BF16), v)
    return m_new, l, acc


def _select_top(score, lane_f, n_sel, n_lanes):
    sel = jnp.zeros(score.shape, F32)
    sc = score
    for _ in range(n_sel):
        mx = jnp.max(sc, axis=1, keepdims=True)
        idx = jnp.min(jnp.where(sc == mx, lane_f, float(n_lanes)), axis=1, keepdims=True)
        pick = lane_f == idx
        sel = jnp.where(pick, 1.0, sel)
        sc = jnp.where(pick, -3e38, sc)
    return sel


def _nsa_compress_kernel(x0_ref, x1_ref, x2_ref, x3_ref, wk_ref, wv_ref, pe_ref, ck_ref, cv_ref, acc_ref):
    a = pl.program_id(1)

    @pl.when(a == 0)
    def _():
        acc_ref[...] = jnp.zeros_like(acc_ref)

    for r in range(8):
        xk = jnp.concatenate([x0_ref[0, :, 0, r, :], x1_ref[0, :, 0, r, :]], axis=1)
        xv = jnp.concatenate([x2_ref[0, :, 0, r, :], x3_ref[0, :, 0, r, :]], axis=1)
        acc_ref[0] += _dot((xk + pe_ref[0, r]).astype(BF16), wk_ref[0, r])
        acc_ref[1] += _dot((xv + pe_ref[1, r]).astype(BF16), wv_ref[0, r])

    @pl.when(a == pl.num_programs(1) - 1)
    def _():
        ck_ref[0] = acc_ref[0]
        cv_ref[0] = acc_ref[1]


def nsa_compress(projb, wbd, pe4):
    b, nb = projb.shape[:2]
    kcol = NSA_Q_COLS // 128
    xspec = lambda c: pl.BlockSpec((1, nb, 1, 8, 128), lambda bi, a: (bi, 0, a, 0, kcol + c))
    return pl.pallas_call(
        _nsa_compress_kernel,
        grid=(b, NSA_BLOCK // 8),
        in_specs=[xspec(0), xspec(1), xspec(2), xspec(3),
                  pl.BlockSpec((1, 8, NSA_KVW, NSA_KVW), lambda bi, a: (0, a, 0, 0)),
                  pl.BlockSpec((1, 8, NSA_KVW, NSA_KVW), lambda bi, a: (1, a, 0, 0)),
                  pl.BlockSpec((2, 8, 1, NSA_KVW), lambda bi, a: (0, a, 0, 0))],
        out_specs=[pl.BlockSpec((1, nb, NSA_KVW), lambda bi, a: (bi, 0, 0)),
                   pl.BlockSpec((1, nb, NSA_KVW), lambda bi, a: (bi, 0, 0))],
        out_shape=[jax.ShapeDtypeStruct((b, nb, NSA_KVW), F32)] * 2,
        scratch_shapes=[pltpu.VMEM((2, nb, NSA_KVW), F32)],
        compiler_params=_cparams(("parallel", "arbitrary")),
        name="nsa_compress",
    )(projb, projb, projb, projb, wbd, wbd, pe4)


LOG2E = 1.4426950408889634
MASK_BIAS = -(2.0 ** 100)
NSA_WTILES = NSA_WINDOW // NSA_QB + 1
NSA_GPP = 2


def _tile_heads(x):
    return jnp.concatenate([x] * NSA_HPG, axis=1)


def _query_columns(q):
    qt = (q * (NSA_SCALE * LOG2E)).T.astype(BF16)
    return jnp.concatenate([qt[h * NSA_DH:(h + 1) * NSA_DH] for h in range(NSA_HPG)], axis=1)


def _nsa_select_kernel(q_ref, ck_ref, cvt_ref, selb_ref, ocmp_ref, *, nb, n_sel):
    i = pl.program_id(1)
    qb, hpg = NSA_QB, NSA_HPG
    gs = range(NSA_GROUPS)
    tq = lax.broadcasted_iota(jnp.int32, (nb, qb), 1) + i * qb
    nblk = lax.broadcasted_iota(jnp.int32, (nb, qb), 0)
    nblk_f = nblk.astype(F32)
    ok_c = _tile_heads((nblk + 1) * NSA_BLOCK - 1 <= tq)
    cur = tq // NSA_BLOCK
    forced = (nblk == 0) | (nblk == cur) | (nblk == cur - 1)
    q_cols = [_query_columns(q_ref[0, :, g * NSA_KVW:(g + 1) * NSA_KVW]) for g in gs]
    s = [jnp.where(ok_c, _dot(ck_ref[0, g], q_cols[g]), NEG) for g in gs]
    p = [jnp.where(ok_c, jnp.exp2(s[g] - jnp.max(s[g], axis=0, keepdims=True)), 0.0) for g in gs]
    p = [p[g] / jnp.maximum(jnp.sum(p[g], axis=0, keepdims=True), 1e-30) for g in gs]
    for g in gs:
        ocmp_ref[0, g, 0] = _dot(cvt_ref[0, g], p[g].astype(BF16))
    imp = [p[g][:, 0:qb] for g in gs]
    for h in range(1, hpg):
        imp = [imp[g] + p[g][:, h * qb:(h + 1) * qb] for g in gs]
    sel = [forced.astype(F32) for g in gs]
    sc = [jnp.where(forced, -3e38, jnp.where(nblk > cur, -1.0, imp[g])) for g in gs]
    for _ in range(n_sel - 3):
        mx = [jnp.max(sc[g], axis=0, keepdims=True) for g in gs]
        idx = [jnp.min(jnp.where(sc[g] == mx[g], nblk_f, float(nb)), axis=0, keepdims=True) for g in gs]
        pick = [nblk_f == idx[g] for g in gs]
        sel = [jnp.where(pick[g], 1.0, sel[g]) for g in gs]
        sc = [jnp.where(pick[g], -3e38, sc[g]) for g in gs]
    for g in gs:
        selb_ref[0, g] = ((sel[g] - 1.0) * (-MASK_BIAS)).astype(BF16)


def nsa_select(proj3, ckg, cvt):
    b, t, _ = proj3.shape
    g, dh = NSA_GROUPS, NSA_DH
    nb = ckg.shape[2]
    n_sel = min(NSA_TOP_N, nb)
    assert n_sel >= 3
    kern = functools.partial(_nsa_select_kernel, nb=nb, n_sel=n_sel)
    return pl.pallas_call(
        kern,
        grid=(b, t // NSA_QB),
        in_specs=[pl.BlockSpec((1, NSA_QB, NSA_Q_COLS), lambda bi, i: (bi, i, 0)),
                  pl.BlockSpec((1, g, nb, dh), lambda bi, i: (bi, 0, 0, 0)),
                  pl.BlockSpec((1, g, dh, nb), lambda bi, i: (bi, 0, 0, 0))],
        out_specs=[pl.BlockSpec((1, g, nb, NSA_QB), lambda bi, i: (bi, 0, 0, i)),
                   pl.BlockSpec((1, g, 1, dh, NSA_HPG * NSA_QB), lambda bi, i: (bi, 0, i, 0, 0))],
        out_shape=[jax.ShapeDtypeStruct((b, g, nb, t), BF16),
                   jax.ShapeDtypeStruct((b, g, t // NSA_QB, dh, NSA_HPG * NSA_QB), F32)],
        compiler_params=_cparams(("parallel", "arbitrary")),
        name="nsa_select",
    )(proj3, ckg, cvt)


def _nsa_prompt_kernel(q_ref, gt_ref, selb_ref, ocmp_ref, oh_ref, kg_ref, vst_ref, vwt_ref, o_ref, acc_ref):
    i = pl.program_id(2)
    qb, dh, hpg, tk = NSA_QB, NSA_DH, NSA_HPG, NSA_TK
    gs = range(NSA_GPP)
    q_cols = [_query_columns(q_ref[0, :, g * NSA_KVW:(g + 1) * NSA_KVW]) for g in gs]
    rhs = [jnp.concatenate([_tile_heads(selb_ref[0, g]), q_cols[g]], axis=0) for g in gs]
    tq_k = lax.broadcasted_iota(jnp.int32, (tk, qb), 1) + i * qb
    krow = lax.broadcasted_iota(jnp.int32, (tk, qb), 0)

    acc_ref[...] = jnp.zeros_like(acc_ref)

    def scores(kt):
        onehot = oh_ref[pl.ds(kt * tk, tk), :]
        return [_dot(jnp.concatenate([onehot, kg_ref[0, g, pl.ds(kt * tk, tk), 0:dh]], axis=1), rhs[g])
                for g in gs]

    def causal(scs, kpos0):
        bias = _tile_heads(jnp.where(krow + kpos0 <= tq_k, 0.0, MASK_BIAS))
        return [sc + bias for sc in scs]

    def absorb(scs, kt, ml):
        out = []
        for g in gs:
            m, l = ml[g]
            m_new = jnp.maximum(m, jnp.max(scs[g], axis=0, keepdims=True))
            alpha = jnp.exp2(m - m_new)
            pt = jnp.exp2(scs[g] - m_new)
            l = alpha * l + jnp.sum(pt, axis=0, keepdims=True)
            acc_ref[g] = alpha * acc_ref[g] + _dot(vst_ref[0, g, kt], pt.astype(BF16))
            out.append((m_new, l))
        return tuple(out)

    def pair(j, ml):
        sa, sb = scores(2 * j), scores(2 * j + 1)
        return absorb(sb, 2 * j + 1, absorb(sa, 2 * j, ml))

    n_tiles = (i * qb + qb + tk - 1) // tk
    n_pairs = (n_tiles - 1) // 2
    ml0 = tuple((jnp.full((1, hpg * qb), NEG, F32), jnp.zeros((1, hpg * qb), F32)) for _ in gs)
    ml1 = lax.fori_loop(0, n_pairs, pair, ml0)
    kt_a = 2 * n_pairs
    kt_d = n_tiles - 1
    sa, sd = scores(kt_a), scores(kt_d)
    sa = causal(sa, jnp.where(kt_a < kt_d, kt_a * tk, 2 * kt_d * tk + 2 * tk))
    sd = causal(sd, kt_d * tk)
    ml2 = absorb(sd, kt_d, absorb(sa, kt_a, ml1))
    o_slc = [acc_ref[g] / jnp.maximum(ml2[g][1], 1e-30) for g in gs]

    w0 = jnp.maximum(i - (NSA_WTILES - 1), 0)
    wl = NSA_WTILES * qb
    dist = (lax.broadcasted_iota(jnp.int32, (wl, qb), 1) + i * qb
            - lax.broadcasted_iota(jnp.int32, (wl, qb), 0) - w0 * qb)
    wbias = _tile_heads(jnp.where((dist >= 0) & (dist < NSA_WINDOW), 0.0, MASK_BIAS))
    sw = [_dot(kg_ref[0, g, pl.ds(w0 * qb, wl), dh:2 * dh], q_cols[g]) + wbias for g in gs]
    pw = [jnp.exp2(sw[g] - jnp.max(sw[g], axis=0, keepdims=True)) for g in gs]
    lw = [jnp.sum(pw[g], axis=0, keepdims=True) for g in gs]
    pw = [pw[g].astype(BF16) for g in gs]
    o_win = [_dot(vwt_ref[0, g, w0], pw[g][0:qb]) for g in gs]
    for jj in range(1, NSA_WTILES):
        o_win = [o_win[g] + _dot(vwt_ref[0, g, w0 + jj], pw[g][jj * qb:(jj + 1) * qb]) for g in gs]
    o_win = [o_win[g] / jnp.maximum(lw[g], 1e-30) for g in gs]

    for g in gs:
        gt = _sigmoid(gt_ref[0, g])
        o_cmp = ocmp_ref[0, g, 0]
        outs = []
        for h in range(hpg):
            cs = slice(h * qb, (h + 1) * qb)
            outs.append(gt[3 * h:3 * h + 1] * o_cmp[:, cs] + gt[3 * h + 1:3 * h + 2] * o_slc[g][:, cs]
                        + gt[3 * h + 2:3 * h + 3] * o_win[g][:, cs])
        o_ref[0, :, g * NSA_KVW:(g + 1) * NSA_KVW] = jnp.concatenate(outs, axis=0).T.astype(o_ref.dtype)


def nsa_prompt_attn(proj3, gtt, selb, ocmp, onehot, kg, vst, vwt):
    b, t, _ = proj3.shape
    g = NSA_GROUPS
    nb = selb.shape[2]
    assert t % NSA_TK == 0 and t >= NSA_WTILES * NSA_QB and g % NSA_GPP == 0
    gp = NSA_GPP
    per_bg = lambda bi, gi, i: (bi, gi, 0, 0)
    return pl.pallas_call(
        _nsa_prompt_kernel,
        grid=(b, g // gp, t // NSA_QB),
        in_specs=[pl.BlockSpec((1, NSA_QB, gp * NSA_KVW), lambda bi, gi, i: (bi, i, gi)),
                  pl.BlockSpec((1, gp, 3 * NSA_HPG, NSA_QB), lambda bi, gi, i: (bi, gi, 0, i)),
                  pl.BlockSpec((1, gp, nb, NSA_QB), lambda bi, gi, i: (bi, gi, 0, i)),
                  pl.BlockSpec((1, gp, 1, NSA_DH, NSA_HPG * NSA_QB), lambda bi, gi, i: (bi, gi, i, 0, 0)),
                  pl.BlockSpec((t, nb), lambda bi, gi, i: (0, 0)),
                  pl.BlockSpec((1, gp, t, 2 * NSA_DH), per_bg),
                  pl.BlockSpec((1, gp, t // NSA_TK, NSA_DH, NSA_TK), lambda bi, gi, i: (bi, gi, 0, 0, 0)),
                  pl.BlockSpec((1, gp, t // NSA_QB, NSA_DH, NSA_QB), lambda bi, gi, i: (bi, gi, 0, 0, 0))],
        out_specs=pl.BlockSpec((1, NSA_QB, gp * NSA_KVW), lambda bi, gi, i: (bi, i, gi)),
        out_shape=jax.ShapeDtypeStruct((b, t, NSA_Q_COLS), BF16),
        scratch_shapes=[pltpu.VMEM((gp, NSA_DH, NSA_HPG * NSA_QB), F32)],
        compiler_params=_cparams(("parallel", "parallel", "arbitrary")),
        name="nsa_prompt_attn",
    )(proj3, gtt, selb, ocmp, onehot, kg, vst, vwt)


def nsa_prompt(x, gain_pre, gain_post, wts, tm):
    w_in, wbd, pe4, w_out = wts
    b, t, d = x.shape
    x2 = x.reshape(b * t, d)
    proj = norm_matmul(x2, gain_pre, w_in, PROJ_TM)
    proj3 = proj.reshape(b, t, NSA_PROJ)
    nb = t // NSA_BLOCK
    ck, cv = nsa_compress(proj.reshape(b, nb, NSA_BLOCK // 8, 8, NSA_PROJ), wbd, pe4)
    g, dh = NSA_GROUPS, NSA_DH
    ckg = ck.reshape(b, nb, g, dh).transpose(0, 2, 1, 3).astype(BF16)
    cvt = cv.reshape(b, nb, g, dh).transpose(0, 2, 3, 1).astype(BF16)
    c0 = NSA_Q_COLS
    kv_rows = proj3[:, :, c0:c0 + 4 * NSA_KVW].reshape(b, t, 4, g, dh)
    wb = min(NSA_WINDOW, t)
    win_rows = proj3[:, t - wb:, c0 + 4 * NSA_KVW:c0 + 6 * NSA_KVW].reshape(b, wb, 2, g, dh)
    kv6 = proj3[:, :, c0 + 2 * NSA_KVW:c0 + 6 * NSA_KVW].reshape(b, t, 4, g, dh).astype(BF16)
    kg = jnp.stack([kv6[:, :, 0], kv6[:, :, 2]], axis=3).transpose(0, 2, 1, 3, 4).reshape(b, g, t, 2 * dh)
    vst = kv6[:, :, 1].reshape(b, t // NSA_TK, NSA_TK, g, dh).transpose(0, 3, 1, 4, 2)
    vwt = kv6[:, :, 3].reshape(b, t // NSA_QB, NSA_QB, g, dh).transpose(0, 3, 1, 4, 2)
    onehot = (jnp.arange(t)[:, None] // NSA_BLOCK == jnp.arange(nb)[None, :]).astype(BF16)
    gtt = proj3[:, :, NSA_GATE_COL:NSA_GATE_COL + 3 * NSA_HEADS].reshape(b, t, g, 3 * NSA_HPG).transpose(0, 2, 3, 1)
    selb, ocmp = nsa_select(proj3, ckg, cvt)
    o = nsa_prompt_attn(proj3, gtt, selb, ocmp, onehot, kg, vst, vwt)
    out = matmul_norm_res(o.reshape(b * t, NSA_Q_COLS), w_out, gain_post, x2, tm)
    return out.reshape(b, t, d), kv_rows, win_rows


def nsa_weights(w_in, cmp_pe, cmp_w, w_out):
    w_in_p = jnp.pad(w_in, ((0, 0), (0, NSA_PROJ - w_in.shape[1]))).astype(BF16)
    eye = jnp.eye(NSA_GROUPS, dtype=F32)
    wbd = jnp.einsum('gh,cpde->cpgdhe', eye, cmp_w).reshape(2, NSA_BLOCK, NSA_KVW, NSA_KVW).astype(BF16)
    pe4 = jnp.tile(cmp_pe.transpose(1, 0, 2)[:, :, None, :], (1, 1, 1, NSA_GROUPS))
    return w_in_p, wbd, pe4, w_out


PAGES_PER_STEP = 8
STG_PITCH = NSA_BLOCK + 4


def _nsa_compress_paged_kernel(pt_ref, *refs, nb, n_steps):
    pages = refs[:PAGES_PER_STEP]
    wk_ref, wv_ref, pe_ref, ck_ref, cv_ref, stg_ref, acc_ref = refs[PAGES_PER_STEP:]
    bi = pl.program_id(0)
    j = pl.program_id(1)
    page_rows = pages[0].shape[2]
    slot = bi % 2
    bpp = page_rows // NSA_BLOCK
    for r in range(PAGES_PER_STEP):
        for c in range(4):
            xt = pages[r][0, c * 128:(c + 1) * 128, :].T
            for k in range(bpp):
                row0 = pl.multiple_of((j * PAGES_PER_STEP + r) * (bpp * STG_PITCH), 8) + k * STG_PITCH
                stg_ref[slot * 4 + c, pl.ds(row0, NSA_BLOCK), :] = xt[k * NSA_BLOCK:(k + 1) * NSA_BLOCK]

    @pl.when(bi > 0)
    def _():
        @pl.when(j == 0)
        def _():
            acc_ref[...] = jnp.zeros_like(acc_ref)

        prev = 1 - slot
        pps = wk_ref.shape[1]
        for r in range(pps):
            p = j * pps + r
            rows_p = pl.ds(p, nb, stride=STG_PITCH)
            xk = jnp.concatenate([stg_ref[prev * 4, rows_p, :], stg_ref[prev * 4 + 1, rows_p, :]], axis=1)
            xv = jnp.concatenate([stg_ref[prev * 4 + 2, rows_p, :], stg_ref[prev * 4 + 3, rows_p, :]], axis=1)
            acc_ref[0] += _dot((xk + pe_ref[0, r]).astype(BF16), wk_ref[0, r])
            acc_ref[1] += _dot((xv + pe_ref[1, r]).astype(BF16), wv_ref[0, r])

        @pl.when(j == n_steps - 1)
        def _():
            ck_ref[0] = acc_ref[0]
            cv_ref[0] = acc_ref[1]


def nsa_compress_paged(cache2, page_table, wbd, pe4):
    b, n_pages = page_table.shape
    page_rows = cache2.shape[2]
    past = n_pages * page_rows
    nb = past // NSA_BLOCK
    n_steps = n_pages // PAGES_PER_STEP
    pps = NSA_BLOCK // n_steps
    assert n_pages % PAGES_PER_STEP == 0 and NSA_BLOCK % n_steps == 0 and page_rows % NSA_BLOCK == 0
    pt = page_table.reshape(-1)

    def page_map(r):
        return lambda bi, j, pt_ref: (pt_ref[jnp.minimum(bi, b - 1) * n_pages + j * PAGES_PER_STEP + r], 0, 0)

    out_map = lambda bi, j, pt_ref: (jnp.maximum(bi - 1, 0), 0, 0)
    kern = functools.partial(_nsa_compress_paged_kernel, nb=nb, n_steps=n_steps)
    grid_spec = pltpu.PrefetchScalarGridSpec(
        num_scalar_prefetch=1,
        grid=(b + 1, n_steps),
        in_specs=[pl.BlockSpec((1, 2 * NSA_KVW, page_rows), page_map(r)) for r in range(PAGES_PER_STEP)]
        + [pl.BlockSpec((1, pps, NSA_KVW, NSA_KVW), lambda bi, j, pt_ref: (0, j, 0, 0)),
           pl.BlockSpec((1, pps, NSA_KVW, NSA_KVW), lambda bi, j, pt_ref: (1, j, 0, 0)),
           pl.BlockSpec((2, pps, 1, NSA_KVW), lambda bi, j, pt_ref: (0, j, 0, 0))],
        out_specs=[pl.BlockSpec((1, nb, NSA_KVW), out_map), pl.BlockSpec((1, nb, NSA_KVW), out_map)],
        scratch_shapes=[pltpu.VMEM((8, nb * STG_PITCH, 128), F32), pltpu.VMEM((2, nb, NSA_KVW), F32)],
    )
    return pl.pallas_call(
        kern,
        grid_spec=grid_spec,
        out_shape=[jax.ShapeDtypeStruct((b, nb, NSA_KVW), F32)] * 2,
        compiler_params=_cparams(("arbitrary", "arbitrary")),
        name="nsa_compress_paged",
    )(pt, *([cache2] * PAGES_PER_STEP), wbd, wbd, pe4)


def _diag_blocks(x, row_g):
    out = jnp.zeros((x.shape[0], NSA_DH), F32)
    for g in range(NSA_GROUPS):
        out = out + jnp.where(row_g == g, x[:, g * NSA_DH:(g + 1) * NSA_DH], 0.0)
    return out


def _nsa_sample_kernel(pt_ref, *refs, nb, ts, past, n_steps):
    pages = refs[:PAGES_PER_STEP]
    (q_ref, gt_ref, ck_ref, cv_ref, new_ref, win_ref, oh_ref, o_ref,
     bias_ref, ocmp_ref, m_ref, l_ref, acc_ref) = refs[PAGES_PER_STEP:]
    j = pl.program_id(1)
    rows = NSA_HEADS * ts
    page_rows = pages[0].shape[2]
    qbd = (q_ref[0] * (NSA_SCALE * LOG2E)).astype(BF16)
    rid = lax.broadcasted_iota(jnp.int32, (rows, 1), 0)
    row_t = rid % ts
    row_g = (rid // ts) % NSA_GROUPS

    @pl.when(j == 0)
    def _():
        nblk = lax.broadcasted_iota(jnp.int32, (rows, nb), 1)
        qpos = past + row_t
        ok_c = (nblk + 1) * NSA_BLOCK - 1 <= qpos
        s = jnp.where(ok_c, _dot_nt(qbd, ck_ref[0].astype(BF16)), NEG)
        p = jnp.where(ok_c, jnp.exp2(s - jnp.max(s, axis=1, keepdims=True)), 0.0)
        p = p / jnp.maximum(jnp.sum(p, axis=1, keepdims=True), 1e-30)
        ocmp_ref[...] = _diag_blocks(_dot(p.astype(BF16), cv_ref[0].astype(BF16)), row_g)
        gt_rows = NSA_GROUPS * ts
        imp = p[0:gt_rows]
        for h in range(1, NSA_HPG):
            imp = imp + p[h * gt_rows:(h + 1) * gt_rows]
        nb1 = lax.broadcasted_iota(jnp.int32, (gt_rows, nb), 1)
        cur = (past + lax.broadcasted_iota(jnp.int32, (gt_rows, nb), 0) % ts) // NSA_BLOCK
        forced = (nb1 == 0) | (nb1 == cur) | (nb1 == cur - 1)
        score = jnp.where(nb1 > cur, -1.0, jnp.where(forced, NSA_FORCE, imp))
        sel = _select_top(score, nb1.astype(F32), NSA_TOP_N - 1, nb)
        selb = jnp.concatenate([((sel - 1.0) * (-MASK_BIAS)).astype(BF16)] * NSA_HPG, axis=0)
        bias = _dot(selb, oh_ref[...])
        for pg in range(bias_ref.shape[0]):
            bias_ref[pg] = bias[:, pg * page_rows:(pg + 1) * page_rows]
        m_ref[...] = jnp.full(m_ref.shape, NEG, F32)
        l_ref[...] = jnp.zeros_like(l_ref)
        acc_ref[...] = jnp.zeros_like(acc_ref)

    pg0 = j * PAGES_PER_STEP
    sc = jnp.concatenate(
        [_dot(qbd, pages[r][0, 0:NSA_KVW, :].astype(BF16)) + bias_ref[pg0 + r] for r in range(PAGES_PER_STEP)],
        axis=1)
    m_old = m_ref[...]
    m_new = jnp.maximum(m_old, jnp.max(sc, axis=1, keepdims=True))
    alpha = jnp.exp2(m_old - m_new)
    p = jnp.exp2(sc - m_new)
    l_ref[...] = alpha * l_ref[...] + jnp.sum(p, axis=1, keepdims=True)
    m_ref[...] = m_new
    pb = p.astype(BF16)
    pv = _dot_nt(pb[:, 0:page_rows], pages[0][0, NSA_KVW:2 * NSA_KVW, :].astype(BF16))
    for r in range(1, PAGES_PER_STEP):
        pv = pv + _dot_nt(pb[:, r * page_rows:(r + 1) * page_rows],
                          pages[r][0, NSA_KVW:2 * NSA_KVW, :].astype(BF16))
    acc_ref[...] = alpha * acc_ref[...] + pv

    @pl.when(j == n_steps - 1)
    def _():
        new = new_ref[0]
        kidx = lax.broadcasted_iota(jnp.int32, (rows, 8), 1)
        new_ok = (kidx <= row_t) & (kidx < ts)
        ms, ls, accs = _online(_dot_nt(qbd, new[:, 0:NSA_KVW].astype(BF16)), new_ok,
                               new[:, NSA_KVW:2 * NSA_KVW].astype(BF16), m_ref[...], l_ref[...], acc_ref[...])
        o_slc = _diag_blocks(accs, row_g) / jnp.maximum(ls, 1e-30)
        win = win_ref[0]
        wl = win.shape[1]
        wr = lax.broadcasted_iota(jnp.int32, (rows, wl), 1)
        dist = row_t + wl - wr
        mw = jnp.full((rows, 1), NEG, F32)
        lw = jnp.zeros((rows, 1), F32)
        aw = jnp.zeros((rows, NSA_KVW), F32)
        sw = jnp.where((dist >= 0) & (dist < NSA_WINDOW), _dot(qbd, win[0:NSA_KVW].astype(BF16)), NEG)
        mw = jnp.max(sw, axis=1, keepdims=True)
        pw = jnp.where((dist >= 0) & (dist < NSA_WINDOW), jnp.exp2(sw - mw), 0.0)
        lw = jnp.sum(pw, axis=1, keepdims=True)
        aw = _dot_nt(pw.astype(BF16), win[NSA_KVW:2 * NSA_KVW].astype(BF16))
        mw, lw, aw = _online(_dot_nt(qbd, new[:, 2 * NSA_KVW:3 * NSA_KVW].astype(BF16)), new_ok,
                             new[:, 3 * NSA_KVW:4 * NSA_KVW].astype(BF16), mw, lw, aw)
        o_win = _diag_blocks(aw, row_g) / jnp.maximum(lw, 1e-30)
        gt = _sigmoid(gt_ref[0])
        o_ref[0] = gt[:, 0:1] * ocmp_ref[...] + gt[:, 1:2] * o_slc + gt[:, 2:3] * o_win


def nsa_sample_attn(cache2, page_table, qbd, gt, ck, cv, new_kv, win2, ts):
    b, n_pages = page_table.shape
    page_rows = cache2.shape[2]
    past = n_pages * page_rows
    nb = past // NSA_BLOCK
    n_steps = n_pages // PAGES_PER_STEP
    rows = NSA_HEADS * ts
    assert n_pages % PAGES_PER_STEP == 0 and past % NSA_BLOCK == 0 and ts <= 8 and nb >= NSA_TOP_N
    pt = page_table.reshape(-1)
    onehot_t = (jnp.arange(nb)[:, None] == jnp.arange(past)[None, :] // NSA_BLOCK).astype(BF16)

    def page_map(r):
        return lambda bi, j, pt_ref: (pt_ref[bi * n_pages + j * PAGES_PER_STEP + r], 1, 0)

    per_b = lambda bi, j, pt_ref: (bi, 0, 0)
    kern = functools.partial(_nsa_sample_kernel, nb=nb, ts=ts, past=past, n_steps=n_steps)
    grid_spec = pltpu.PrefetchScalarGridSpec(
        num_scalar_prefetch=1,
        grid=(b, n_steps),
        in_specs=[pl.BlockSpec((1, 2 * NSA_KVW, page_rows), page_map(r)) for r in range(PAGES_PER_STEP)]
        + [pl.BlockSpec((1, rows, NSA_KVW), per_b), pl.BlockSpec((1, rows, 3), per_b),
           pl.BlockSpec((1, nb, NSA_KVW), per_b), pl.BlockSpec((1, nb, NSA_KVW), per_b),
           pl.BlockSpec((1, 8, 4 * NSA_KVW), per_b),
           pl.BlockSpec((1, 2 * NSA_KVW, win2.shape[2]), per_b),
           pl.BlockSpec((nb, past), lambda bi, j, pt_ref: (0, 0))],
        out_specs=pl.BlockSpec((1, rows, NSA_DH), per_b),
        scratch_shapes=[pltpu.VMEM((n_pages, rows, page_rows), F32), pltpu.VMEM((rows, NSA_DH), F32),
                        pltpu.VMEM((rows, 1), F32), pltpu.VMEM((rows, 1), F32),
                        pltpu.VMEM((rows, NSA_KVW), F32)],
    )
    return pl.pallas_call(
        kern,
        grid_spec=grid_spec,
        out_shape=jax.ShapeDtypeStruct((b, rows, NSA_DH), F32),
        compiler_params=_cparams(("parallel", "arbitrary")),
        name="nsa_sample_attn",
    )(pt, *([cache2] * PAGES_PER_STEP), qbd, gt, ck, cv, new_kv, win2, onehot_t)


def _win_shift_kernel(win_ref, new_ref, o_ref, *, ts):
    w, wl = win_ref.shape[1], win_ref.shape[2]
    rolled = pltpu.roll(win_ref[0], wl - ts, axis=1)
    new_t = jnp.concatenate([new_ref[0], jnp.zeros((128 - 8, w), F32)], axis=0).T
    new_t = pltpu.roll(new_t, 128 - ts, axis=1)
    lane = lax.broadcasted_iota(jnp.int32, (w, 128), 1)
    o_ref[0, :, 0:wl - 128] = rolled[:, 0:wl - 128]
    o_ref[0, :, wl - 128:wl] = jnp.where(lane >= 128 - ts, new_t, rolled[:, wl - 128:wl])


def win_shift(win2, new_win, ts):
    b, w, wl = win2.shape
    return pl.pallas_call(
        functools.partial(_win_shift_kernel, ts=ts),
        grid=(b,),
        in_specs=[pl.BlockSpec((1, w, wl), lambda bi: (bi, 0, 0)),
                  pl.BlockSpec((1, 8, w), lambda bi: (bi, 0, 0))],
        out_specs=pl.BlockSpec((1, w, wl), lambda bi: (bi, 0, 0)),
        out_shape=jax.ShapeDtypeStruct((b, w, wl), F32),
        compiler_params=_cparams(("parallel",)),
        name="win_shift",
    )(win2, new_win)


def nsa_sample(x, gain_pre, gain_post, cache_kv, cache_win, page_table, wts):
    w_in, wbd, pe4, w_out = wts
    b, ts, d = x.shape
    g, hpg, dh = NSA_GROUPS, NSA_HPG, NSA_DH
    x2 = x.reshape(b * ts, d)
    proj = norm_matmul(x2, gain_pre, w_in, PROJ_TM).reshape(b, ts, NSA_PROJ)
    c0 = NSA_Q_COLS
    kv_rows = proj[:, :, c0:c0 + 4 * NSA_KVW].reshape(b, ts, 4, g, dh)
    n_pool, page_rows = cache_kv.shape[:2]
    cache2 = cache_kv.transpose(0, 2, 3, 4, 1).reshape(n_pool, 4 * NSA_KVW, page_rows)
    wl = cache_win.shape[1]
    win2 = cache_win.transpose(0, 2, 3, 4, 1).reshape(b, 2 * NSA_KVW, wl)
    new_kv = jnp.pad(proj[:, :, c0 + 2 * NSA_KVW:c0 + 6 * NSA_KVW], ((0, 0), (0, 8 - ts), (0, 0)))
    win_rows = win_shift(win2, new_kv[:, :, 2 * NSA_KVW:], ts).reshape(b, 2, g, dh, wl).transpose(0, 4, 1, 2, 3)
    ck, cv = nsa_compress_paged(cache2, page_table, wbd, pe4)
    q = proj[:, :, :c0].reshape(b, ts, g, hpg, dh).transpose(0, 3, 2, 1, 4)
    qbd = jnp.einsum('bhgtd,gk->bhgtkd', q, jnp.eye(g, dtype=F32)).reshape(b, hpg * g * ts, g * dh)
    gt = proj[:, :, NSA_GATE_COL:NSA_GATE_COL + 3 * NSA_HEADS].reshape(b, ts, g, hpg, 3)
    gt = gt.transpose(0, 3, 2, 1, 4).reshape(b, hpg * g * ts, 3)
    o = nsa_sample_attn(cache2, page_table, qbd, gt, ck, cv, new_kv, win2, ts)
    o = o.reshape(b, hpg, g, ts, dh).transpose(0, 3, 2, 1, 4).reshape(b * ts, NSA_Q_COLS).astype(BF16)
    out = matmul_norm_res(o, w_out, gain_post, x2, b * ts)
    return out.reshape(b, ts, d), kv_rows, win_rows


def kernel(x_prompt, x_sample, state_l0_lru_conv, state_l0_lru_h, cache_l1_kv, cache_l1_win, state_l2_gdn_conv, state_l2_gdn_s, state_l3_lru_conv, state_l3_lru_h, page_table, norm_gains, ffn_w_gate_up, ffn_w_down, l0_w_in, l0_conv_w, l0_conv_b, l0_gate_w, l0_gate_b, l0_lam, l0_w_out, l1_w_in, l1_cmp_pe, l1_cmp_w, l1_w_out, l2_w_in, l2_conv_w, l2_a_log, l2_dt_bias, l2_o_norm_w, l2_w_out, l3_w_in, l3_conv_w, l3_conv_b, l3_gate_w, l3_gate_b, l3_lam, l3_w_out):
    bp, tp, d = x_prompt.shape
    bs, ts, _ = x_sample.shape
    tm_p = 1024
    tm_s = bs * ts
    lru_w = {
        0: (l0_w_in.astype(BF16), l0_conv_w, l0_conv_b, l0_gate_w.astype(BF16), l0_gate_b, l0_lam, l0_w_out),
        3: (l3_w_in.astype(BF16), l3_conv_w, l3_conv_b, l3_gate_w.astype(BF16), l3_gate_b, l3_lam, l3_w_out),
    }
    lru_s = {0: (state_l0_lru_conv, state_l0_lru_h), 3: (state_l3_lru_conv, state_l3_lru_h)}
    nsa_w = nsa_weights(l1_w_in, l1_cmp_pe, l1_cmp_w, l1_w_out)
    gdn_w = (jnp.pad(l2_w_in, ((0, 0), (0, GDN_PROJ - l2_w_in.shape[1]))).astype(BF16), l2_conv_w, l2_a_log,
             l2_dt_bias, l2_o_norm_w, l2_w_out)
    xp, xs = x_prompt, x_sample
    new = {}
    for layer in range(4):
        g = norm_gains[layer]
        kind = layer % 3
        if kind == 0:
            conv_s, h_s = lru_s[layer]
            xp, cp, hp = lru_mixer(xp, g[0], g[1], jnp.zeros((bp, CONV_W - 1, d), F32), jnp.zeros((bp, d), F32),
                                   lru_w[layer], tm_p, 256)
            xs, cs, hs = lru_mixer(xs, g[0], g[1], conv_s, h_s, lru_w[layer], tm_s, 256)
            new[layer] = (cp, hp, cs, hs)
        elif kind == 1:
            xp, kv_p, win_p = nsa_prompt(xp, g[0], g[1], nsa_w, tm_p)
            xs, kv_s, win_s = nsa_sample(xs, g[0], g[1], cache_l1_kv, cache_l1_win, page_table, nsa_w)
            new[layer] = (kv_p, win_p, kv_s, win_s)
        else:
            xp, cp, sp = gdn_mixer(xp, g[0], g[1], jnp.zeros((bp, CONV_W - 1, 3 * GDN_W), F32),
                                   jnp.zeros((bp, GDN_HEADS, GDN_DH, GDN_DH), F32), gdn_w, tm_p, 256)
            xs, cs, ss = gdn_mixer(xs, g[0], g[1], state_l2_gdn_conv, state_l2_gdn_s, gdn_w, tm_s, 8)
            new[layer] = (cp, sp, cs, ss)
        wgu = ffn_w_gate_up[layer]
        wd = ffn_w_down[layer]
        xp = ffn(xp.reshape(bp * tp, d), g[2], g[3], wgu, wd, tm_p).reshape(bp, tp, d)
        xs = ffn(xs.reshape(bs * ts, d), g[2], g[3], wgu, wd, tm_s).reshape(bs, ts, d)
    return (xp, xs) + new[0] + new[1] + new[2] + new[3]
```
